```python
import math
import jax, jax.numpy as jnp
from jax import lax
import numpy as np

D_MODEL = 1024
BATCH = 8
SEQ = 2048
DEPTH = 4

POOL_WINDOWS = (2, 4, 8, 16)
POOL_WIDTH = 512
POOL_GROUP = POOL_WIDTH // len(POOL_WINDOWS)
ATT_HEADS = 8
HEAD_DIM = 64
ATT_WIDTH = ATT_HEADS * HEAD_DIM
MOBA_BLOCK = 256
MOBA_TOPK = 3
Q_CHUNK = 16
SSM_WIDTH = 512
SSM_GROUP = 16
SSM_GROUPS = SSM_WIDTH // SSM_GROUP
SSM_STATE = 64
DT_MIN = 1e-3
DT_MAX = 1e-1
N_BRANCH = 3
BRANCH_WIDTH = 512
IN_WIDTH = POOL_WIDTH + 3 * ATT_WIDTH + SSM_WIDTH + N_BRANCH * D_MODEL
D_FF = 2816
N_EXPERTS = 8
TOP_K = 2
N_DENSE = (DEPTH + 1) // 2
N_MOE = DEPTH // 2
DN_ALPHA = (2 * DEPTH) ** 0.25
DN_BETA = (8 * DEPTH) ** -0.25
LN_EPS = 1e-5

kernel_name = "hybrid_gated_pool_moba_s5_moe"


def layer_norm(x, g, b):
    xf = x.astype(jnp.float32)
    mu = jnp.mean(xf, axis=-1, keepdims=True)
    var = jnp.mean(jnp.square(xf - mu), axis=-1, keepdims=True)
    y = (xf - mu) * lax.rsqrt(var + LN_EPS) * g.astype(jnp.float32) + b.astype(jnp.float32)
    return y.astype(x.dtype)


def pool_mixer(u, w_pool, pool_scale):
    S_ = u.shape[1]
    uf = u.astype(jnp.float32)
    cs = jnp.cumsum(uf, axis=1)
    cs = jnp.concatenate([jnp.zeros_like(cs[:, :1]), cs], axis=1)
    t = jnp.arange(S_)
    outs = []
    for gi, w in enumerate(POOL_WINDOWS):
        c = cs[..., gi * POOL_GROUP:(gi + 1) * POOL_GROUP]
        start = jnp.maximum(t + 1 - w, 0)
        count = (t + 1 - start).astype(jnp.float32)[:, None]
        mean = (c[:, 1:] - c[:, start]) / count
        outs.append(mean - uf[..., gi * POOL_GROUP:(gi + 1) * POOL_GROUP])
    z = jnp.stack(outs, axis=2)
    z = jnp.einsum('bsgc,gcd->bsgd', z, w_pool.astype(jnp.float32))
    y = z.reshape(u.shape) * pool_scale.astype(jnp.float32)
    return y.astype(u.dtype)


def moba_attention(q, k, v):
    B_, S_, H, dh = q.shape
    nb = -(-S_ // MOBA_BLOCK)
    pad = nb * MOBA_BLOCK - S_
    q = q.transpose(0, 2, 1, 3) * (dh ** -0.5)
    k = jnp.pad(k.transpose(0, 2, 1, 3), ((0, 0), (0, 0), (0, pad), (0, 0)))
    v = jnp.pad(v.transpose(0, 2, 1, 3), ((0, 0), (0, 0), (0, pad), (0, 0)))
    kb = k.reshape(B_, H, nb, MOBA_BLOCK, dh)
    vb = v.reshape(B_, H, nb, MOBA_BLOCK, dh)
    k_mean = jnp.mean(kb.astype(jnp.float32), axis=3)
    t = jnp.arange(S_)
    q_blk = t // MOBA_BLOCK
    gate = jnp.einsum('bhtd,bhnd->bhtn', q.astype(jnp.float32), k_mean)
    past = jnp.arange(nb)[None, :] < q_blk[:, None]
    gate = jnp.where(past, gate, -jnp.inf)
    k_sel = min(MOBA_TOPK, nb)
    _, sel = lax.top_k(gate, k_sel)
    sel_valid = jnp.arange(k_sel)[None, :] < q_blk[:, None]
    slopes = jnp.exp2(-8.0 * jnp.arange(1, H + 1, dtype=jnp.float32) / H)
    b_idx = jnp.arange(B_)[:, None, None, None]
    h_idx = jnp.arange(H)[None, :, None, None]

    def chunk(c):
        t0 = c * Q_CHUNK
        qc = lax.dynamic_slice_in_dim(q, t0, Q_CHUNK, axis=2)
        sc = lax.dynamic_slice_in_dim(sel, t0, Q_CHUNK, axis=2)
        vc = lax.dynamic_slice_in_dim(sel_valid, t0, Q_CHUNK, axis=0)
        tq = t0 + jnp.arange(Q_CHUNK)
        kg = kb[b_idx, h_idx, sc]
        vg = vb[b_idx, h_idx, sc]
        s_pos = sc[..., None] * MOBA_BLOCK + jnp.arange(MOBA_BLOCK)
        dist = (tq[None, None, :, None, None] - s_pos).astype(jnp.float32)
        s_past = jnp.einsum('bhqd,bhqnkd->bhqnk', qc, kg).astype(jnp.float32)
        s_past = s_past - slopes[None, :, None, None, None] * dist
        s_past = jnp.where(vc[None, None, :, :, None], s_past, -jnp.inf)
        s_past = s_past.reshape(B_, H, Q_CHUNK, k_sel * MOBA_BLOCK)
        j0 = (t0 // MOBA_BLOCK) * MOBA_BLOCK
        ko = lax.dynamic_slice_in_dim(k, j0, MOBA_BLOCK, axis=2)
        vo = lax.dynamic_slice_in_dim(v, j0, MOBA_BLOCK, axis=2)
        so_pos = j0 + jnp.arange(MOBA_BLOCK)
        dist_o = (tq[:, None] - so_pos[None, :]).astype(jnp.float32)
        s_own = jnp.einsum('bhqd,bhkd->bhqk', qc, ko).astype(jnp.float32)
        s_own = s_own - slopes[None, :, None, None] * dist_o
        s_own = jnp.where((so_pos[None, :] <= tq[:, None]), s_own, -jnp.inf)
        p = jax.nn.softmax(jnp.concatenate([s_past, s_own], axis=-1), axis=-1)
        p_past = p[..., :k_sel * MOBA_BLOCK].reshape(B_, H, Q_CHUNK, k_sel, MOBA_BLOCK).astype(v.dtype)
        p_own = p[..., k_sel * MOBA_BLOCK:].astype(v.dtype)
        out = jnp.einsum('bhqnk,bhqnkd->bhqd', p_past, vg) + jnp.einsum('bhqk,bhkd->bhqd', p_own, vo)
        return out.astype(q.dtype)

    outs = lax.map(chunk, jnp.arange(S_ // Q_CHUNK))
    return outs.transpose(1, 0, 3, 2, 4).reshape(B_, S_, H, dh)


def s5_mixer(u, a_re, a_im, log_dt, b_re, b_im, c_re, c_im, d_skip, w_glu, b_glu):
    B_, S_, _ = u.shape
    f32 = jnp.float32
    ug = u.astype(f32).reshape(B_, S_, SSM_GROUPS, SSM_GROUP)
    dt = jnp.exp(log_dt.astype(f32))[:, None]
    ar, ai = a_re.astype(f32), a_im.astype(f32)
    mag = jnp.exp(ar * dt)
    lr, li = mag * jnp.cos(ai * dt), mag * jnp.sin(ai * dt)
    den = ar * ar + ai * ai
    nr = lr - 1.0
    fr = (nr * ar + li * ai) / den
    fi = (li * ar - nr * ai) / den
    br, bi = b_re.astype(f32), b_im.astype(f32)
    bbr = fr[..., None] * br - fi[..., None] * bi
    bbi = fr[..., None] * bi + fi[..., None] * br
    xr = jnp.einsum('bsgp,gnp->bsgn', ug, bbr)
    xi = jnp.einsum('bsgp,gnp->bsgn', ug, bbi)
    lr_t = jnp.broadcast_to(lr, xr.shape)
    li_t = jnp.broadcast_to(li, xr.shape)

    def combine(e1, e2):
        a1r, a1i, h1r, h1i = e1
        a2r, a2i, h2r, h2i = e2
        return (a2r * a1r - a2i * a1i, a2r * a1i + a2i * a1r,
                a2r * h1r - a2i * h1i + h2r, a2r * h1i + a2i * h1r + h2i)

    _, _, hr, hi = lax.associative_scan(combine, (lr_t, li_t, xr, xi), axis=1)
    y = (jnp.einsum('bsgn,gpn->bsgp', hr, c_re.astype(f32))
         - jnp.einsum('bsgn,gpn->bsgp', hi, c_im.astype(f32))
         + d_skip.astype(f32).reshape(SSM_GROUPS, SSM_GROUP) * ug)
    y = jax.nn.gelu(y.reshape(B_, S_, SSM_WIDTH))
    z = y @ w_glu.astype(f32) + b_glu.astype(f32)
    out = z[..., :SSM_WIDTH] * jax.nn.sigmoid(z[..., SSM_WIDTH:])
    return out.astype(u.dtype)


def mixer_sublayer(x, w_in, b_in, w_pool, pool_scale, a_re, a_im, log_dt, b_re, b_im,
                   c_re, c_im, d_skip, w_glu, b_glu, w_branch, w_out):
    B_, S_, D = x.shape
    z = x @ w_in + b_in
    cuts = [POOL_WIDTH, POOL_WIDTH + ATT_WIDTH, POOL_WIDTH + 2 * ATT_WIDTH,
            POOL_WIDTH + 3 * ATT_WIDTH, POOL_WIDTH + 3 * ATT_WIDTH + SSM_WIDTH]
    u_a, q, k, v, u_c, g = jnp.split(z, cuts, axis=-1)
    y_a = pool_mixer(u_a, w_pool, pool_scale)
    hs = (B_, S_, ATT_HEADS, HEAD_DIM)
    y_b = moba_attention(q.reshape(hs), k.reshape(hs), v.reshape(hs)).reshape(B_, S_, ATT_WIDTH)
    y_c = s5_mixer(u_c, a_re, a_im, log_dt, b_re, b_im, c_re, c_im, d_skip, w_glu, b_glu)
    ys = jnp.stack([y_a, y_b, y_c], axis=2)
    proj = jnp.einsum('bsnc,ncd->bsnd', ys, w_branch)
    gates = jax.nn.sigmoid(g.reshape(B_, S_, N_BRANCH, D))
    merged = jnp.sum(gates * proj, axis=2)
    return merged @ w_out


def swiglu(x, w1, w3, w2):
    return (jax.nn.silu(x @ w1) * (x @ w3)) @ w2


def moe_swiglu(x, router, router_b, w1, w3, w2):
    logits = (x @ router + router_b).astype(jnp.float32)
    top_val, top_idx = lax.top_k(logits, TOP_K)
    top_w = jax.nn.softmax(top_val, axis=-1)
    gate = jnp.sum(jax.nn.one_hot(top_idx, N_EXPERTS, dtype=jnp.float32) * top_w[..., None], axis=-2)
    y = jnp.zeros_like(x)
    for e in range(N_EXPERTS):
        y = y + gate[..., e:e + 1].astype(x.dtype) * swiglu(x, w1[e], w3[e], w2[e])
    return y


def setup_inputs(seed: int = 0) -> dict:
    key = jax.random.key(seed)
    ks = iter(jax.random.split(key, 40))
    f32 = jnp.float32

    def nrm(shape, scale):
        return jax.random.normal(next(ks), shape, f32) * scale

    G, N, P = SSM_GROUPS, SSM_STATE, SSM_GROUP
    n_idx = jnp.arange(N, dtype=f32)[None, None, :]
    return {
        "x": nrm((BATCH, SEQ, D_MODEL), 1.0),
        "ln1_g": 1.0 + nrm((DEPTH, D_MODEL), 0.02),
        "ln1_b": nrm((DEPTH, D_MODEL), 0.02),
        "w_in": nrm((DEPTH, D_MODEL, IN_WIDTH), D_MODEL ** -0.5),
        "b_in": nrm((DEPTH, IN_WIDTH), 0.01),
        "w_pool": nrm((DEPTH, len(POOL_WINDOWS), POOL_GROUP, POOL_GROUP), POOL_GROUP ** -0.5),
        "pool_scale": 1.0 + nrm((DEPTH, POOL_WIDTH), 0.02),
        "ssm_a_re": -0.5 + nrm((DEPTH, G, N), 0.01),
        "ssm_a_im": math.pi * n_idx + nrm((DEPTH, G, N), 0.01),
        "ssm_log_dt": jax.random.uniform(next(ks), (DEPTH, G), f32, math.log(DT_MIN), math.log(DT_MAX)),
        "ssm_b_re": nrm((DEPTH, G, N, P), (2 * P) ** -0.5),
        "ssm_b_im": nrm((DEPTH, G, N, P), (2 * P) ** -0.5),
        "ssm_c_re": nrm((DEPTH, G, P, N), (2 * N) ** -0.5),
        "ssm_c_im": nrm((DEPTH, G, P, N), (2 * N) ** -0.5),
        "ssm_d": nrm((DEPTH, SSM_WIDTH), 1.0),
        "w_glu": nrm((DEPTH, SSM_WIDTH, 2 * SSM_WIDTH), SSM_WIDTH ** -0.5),
        "b_glu": nrm((DEPTH, 2 * SSM_WIDTH), 0.01),
        "w_branch": nrm((DEPTH, N_BRANCH, BRANCH_WIDTH, D_MODEL), BRANCH_WIDTH ** -0.5),
        "w_out": nrm((DEPTH, D_MODEL, D_MODEL), D_MODEL ** -0.5 * DN_BETA),
        "ln2_g": 1.0 + nrm((DEPTH, D_MODEL), 0.02),
        "ln2_b": nrm((DEPTH, D_MODEL), 0.02),
        "ffn_w1": nrm((N_DENSE, D_MODEL, D_FF), D_MODEL ** -0.5),
        "ffn_w3": nrm((N_DENSE, D_MODEL, D_FF), D_MODEL ** -0.5),
        "ffn_w2": nrm((N_DENSE, D_FF, D_MODEL), D_FF ** -0.5 * DN_BETA),
        "moe_router": nrm((N_MOE, D_MODEL, N_EXPERTS), D_MODEL ** -0.5),
        "moe_router_b": nrm((N_MOE, N_EXPERTS), 0.01),
        "moe_w1": nrm((N_MOE, N_EXPERTS, D_MODEL, D_FF), D_MODEL ** -0.5),
        "moe_w3": nrm((N_MOE, N_EXPERTS, D_MODEL, D_FF), D_MODEL ** -0.5),
        "moe_w2": nrm((N_MOE, N_EXPERTS, D_FF, D_MODEL), D_FF ** -0.5 * DN_BETA),
    }


def reference(x, ln1_g, ln1_b, w_in, b_in, w_pool, pool_scale, ssm_a_re, ssm_a_im, ssm_log_dt,
              ssm_b_re, ssm_b_im, ssm_c_re, ssm_c_im, ssm_d, w_glu, b_glu, w_branch, w_out,
              ln2_g, ln2_b, ffn_w1, ffn_w3, ffn_w2, moe_router, moe_router_b, moe_w1, moe_w3, moe_w2):
    h = x
    for i in range(DEPTH):
        mix = mixer_sublayer(h, w_in[i], b_in[i], w_pool[i], pool_scale[i], ssm_a_re[i], ssm_a_im[i],
                             ssm_log_dt[i], ssm_b_re[i], ssm_b_im[i], ssm_c_re[i], ssm_c_im[i], ssm_d[i],
                             w_glu[i], b_glu[i], w_branch[i], w_out[i])
        h = layer_norm(DN_ALPHA * h + mix, ln1_g[i], ln1_b[i])
        j = i // 2
        if i % 2 == 0:
            f = swiglu(h, ffn_w1[j], ffn_w3[j], ffn_w2[j])
        else:
            f = moe_swiglu(h, moe_router[j], moe_router_b[j], moe_w1[j], moe_w3[j], moe_w2[j])
        h = layer_norm(DN_ALPHA * h + f, ln2_g[i], ln2_b[i])
    return h
```

```python
import functools
import math

import jax
import jax.numpy as jnp
from jax import lax
from jax.experimental import pallas as pl
from jax.experimental.pallas import tpu as pltpu

F32 = jnp.float32
BF16 = jnp.bfloat16

D_MODEL = 1024
DEPTH = 4
POOL_WINDOWS = (2, 4, 8, 16)
POOL_WIDTH = 512
POOL_GROUP = POOL_WIDTH // len(POOL_WINDOWS)
ATT_HEADS = 8
HEAD_DIM = 64
ATT_WIDTH = ATT_HEADS * HEAD_DIM
MOBA_BLOCK = 256
MOBA_TOPK = 3
SSM_WIDTH = 512
SSM_GROUP = 16
SSM_GROUPS = SSM_WIDTH // SSM_GROUP
SSM_STATE = 64
N_BRANCH = 3
BRANCH_WIDTH = 512
IN_WIDTH = POOL_WIDTH + 3 * ATT_WIDTH + SSM_WIDTH + N_BRANCH * D_MODEL
D_FF = 2816
N_EXPERTS = 8
TOP_K = 2
DN_ALPHA = (2 * DEPTH) ** 0.25
LN_EPS = 1e-5

LANES = 128
SUBLANES = 8
VMEM_LIMIT = 48 * 1024 * 1024

NEG = -1e30

COL_POOL, COL_Q, COL_K, COL_V, COL_SSM, COL_GATE = 0, 1, 2, 3, 4, 5
Z_BLOCKS = IN_WIDTH // 512


def _cparams(*sem):
    return pltpu.CompilerParams(dimension_semantics=sem, vmem_limit_bytes=VMEM_LIMIT)


def _layer_norm(r, g, b):
    mu = jnp.mean(r, axis=-1, keepdims=True)
    c = r - mu
    var = jnp.mean(c * c, axis=-1, keepdims=True)
    return c * lax.rsqrt(var + LN_EPS) * g + b


def _inproj_kernel(x_ref, w_ref, b_ref, o_ref, xb_ref):
    @pl.when(pl.program_id(1) == 0)
    def _():
        xb_ref[...] = x_ref[...].astype(BF16)

    acc = jnp.dot(xb_ref[...], w_ref[...], preferred_element_type=F32)
    o_ref[...] = (acc + b_ref[...]).astype(o_ref.dtype)


def _inproj(x, w, b):
    T, K = x.shape
    N = w.shape[1]
    tm, tn = 1024, 512
    return pl.pallas_call(
        _inproj_kernel,
        grid=(T // tm, N // tn),
        in_specs=[
            pl.BlockSpec((tm, K), lambda i, j: (i, 0)),
            pl.BlockSpec((K, tn), lambda i, j: (0, j)),
            pl.BlockSpec((1, tn), lambda i, j: (0, j)),
        ],
        out_specs=pl.BlockSpec((tm, tn), lambda i, j: (i, j)),
        out_shape=jax.ShapeDtypeStruct((T, N), BF16),
        scratch_shapes=[pltpu.VMEM((tm, K), BF16)],
        compiler_params=_cparams("arbitrary", "arbitrary"),
        name="inproj",
    )(x, w, b)


def _attn_kernel(q_ref, k_ref, v_ref, o_ref, kmh_ref, kml_ref, s_scr, macc, lacc, acc_ref,
                 *, nb, blk, dh, n_heads, topk):
    pair = pl.program_id(1)
    i = pl.program_id(2)
    W = 2 * dh
    lane = lax.broadcasted_iota(jnp.int32, (1, W), 1)
    head_of_lane = lane // dh
    nt = (((1,), (1,)), ((), ()))

    @pl.when(i == 0)
    def _():
        kmh_ref[...] = jnp.zeros_like(kmh_ref)
        kml_ref[...] = jnp.zeros_like(kml_ref)
        for n in range(nb):
            kb = k_ref[n * blk:(n + 1) * blk, :].astype(F32)
            mean = jnp.sum(kb, axis=0, keepdims=True) * (1.0 / blk)
            for hl in range(2):
                row = jnp.where(head_of_lane == hl, mean, 0.0)
                hi = row.astype(BF16)
                lo = (row - hi.astype(F32)).astype(BF16)
                kmh_ref[hl * nb + n:hl * nb + n + 1, :] = hi
                kml_ref[hl * nb + n:hl * nb + n + 1, :] = lo

    q = q_ref[...] * jnp.asarray(dh ** -0.5, BF16)
    gate_t = (lax.dot_general(kmh_ref[...], q, nt, preferred_element_type=F32)
              + lax.dot_general(kml_ref[...], q, nt, preferred_element_type=F32))

    row = lax.broadcasted_iota(jnp.int32, (nb, blk), 0)
    rowdist = ((row - i) * blk).astype(F32)
    slopes = []
    bias_parts = []
    for hl in range(2):
        h_f = (pair * 2 + hl + 1).astype(F32)
        slope = jnp.exp2(jnp.full((1, blk), -8.0 / n_heads, F32) * h_f)
        slopes.append(slope)
        g = gate_t[hl * nb:(hl + 1) * nb, :]
        cnt = jnp.zeros((nb, blk), F32)
        for n2 in range(nb):
            r = g[n2:n2 + 1, :]
            beats = jnp.where(r > g, 1.0, jnp.where(r == g, jnp.where(n2 < row, 1.0, 0.0), 0.0))
            cnt = cnt + jnp.where(n2 < i, beats, 0.0)
        sel = jnp.where(row < i, jnp.where(cnt < float(topk), 1.0, 0.0), 0.0)
        bias_parts.append(jnp.where(sel > 0.5, slope * rowdist, NEG))
    bias_t = jnp.concatenate(bias_parts + [jnp.zeros((LANES - 2 * nb, blk), F32)], axis=0)
    bias = bias_t.T

    key_pos = lax.broadcasted_iota(jnp.int32, (1, blk), 1).astype(F32)
    r_io = lax.broadcasted_iota(jnp.int32, (blk, blk), 0)
    c_io = lax.broadcasted_iota(jnp.int32, (blk, blk), 1)
    own0 = pl.multiple_of(i * blk, blk)

    outs = []
    for hl in range(2):
        qh = jnp.where(head_of_lane == hl, q, jnp.zeros_like(q))
        keybias = slopes[hl] * key_pos

        s = lax.dot_general(qh, k_ref[pl.ds(own0, blk), :], nt, preferred_element_type=F32)
        s = jnp.where(c_io <= r_io, s + keybias, NEG)
        s_scr[:, 0:blk] = s
        macc[...] = jnp.maximum(s[:, :LANES], s[:, LANES:])

        for n in range(nb - 1):
            @pl.when(n < i)
            def _(n=n, hl=hl, qh=qh, keybias=keybias):
                sn = lax.dot_general(qh, k_ref[n * blk:(n + 1) * blk, :], nt,
                                     preferred_element_type=F32)
                c = hl * nb + n
                sn = sn + keybias + bias[:, c:c + 1]
                s_scr[:, (n + 1) * blk:(n + 2) * blk] = sn
                macc[...] = jnp.maximum(macc[...], jnp.maximum(sn[:, :LANES], sn[:, LANES:]))

        m = jnp.max(macc[...], axis=1, keepdims=True)

        p0 = jnp.exp(s_scr[:, 0:blk] - m)
        lacc[...] = p0[:, :LANES] + p0[:, LANES:]
        acc_ref[...] = jnp.dot(p0.astype(BF16), v_ref[pl.ds(own0, blk), :],
                               preferred_element_type=F32)
        for n in range(nb - 1):
            @pl.when(n < i)
            def _(n=n, m=m):
                pn = jnp.exp(s_scr[:, (n + 1) * blk:(n + 2) * blk] - m)
                lacc[...] += pn[:, :LANES] + pn[:, LANES:]
                acc_ref[...] += jnp.dot(pn.astype(BF16), v_ref[n * blk:(n + 1) * blk, :],
                                        preferred_element_type=F32)
        l = jnp.sum(lacc[...], axis=1, keepdims=True)
        outs.append(acc_ref[...] / l)

    o_ref[...] = jnp.where(head_of_lane == 0, outs[0], outs[1]).astype(o_ref.dtype)


def _attention(z2, B, S):
    blk, dh = MOBA_BLOCK, HEAD_DIM
    nb = S // blk
    assert nb * blk == S and nb == SUBLANES and 2 * dh == LANES
    n_pairs = ATT_HEADS // 2
    per_b = IN_WIDTH // LANES
    qc, kc, vc = (COL_Q * 512 // LANES, COL_K * 512 // LANES, COL_V * 512 // LANES)
    kern = functools.partial(_attn_kernel, nb=nb, blk=blk, dh=dh, n_heads=ATT_HEADS, topk=MOBA_TOPK)
    return pl.pallas_call(
        kern,
        grid=(B, n_pairs, nb),
        in_specs=[
            pl.BlockSpec((blk, LANES), lambda b, p, i: (i, b * per_b + qc + p)),
            pl.BlockSpec((S, LANES), lambda b, p, i: (0, b * per_b + kc + p)),
            pl.BlockSpec((S, LANES), lambda b, p, i: (0, b * per_b + vc + p)),
        ],
        out_specs=pl.BlockSpec((blk, LANES), lambda b, p, i: (i, b * n_pairs + p)),
        out_shape=jax.ShapeDtypeStruct((S, B * ATT_WIDTH), BF16),
        scratch_shapes=[
            pltpu.VMEM((LANES, LANES), BF16),
            pltpu.VMEM((LANES, LANES), BF16),
            pltpu.VMEM((blk, nb * blk), F32),
            pltpu.VMEM((blk, LANES), F32),
            pltpu.VMEM((blk, LANES), F32),
            pltpu.VMEM((blk, LANES), F32),
        ],
        compiler_params=_cparams("arbitrary", "arbitrary", "arbitrary"),
        name="moba_attn",
    )(z2, z2, z2)


POOL_HALO_STEPS = max(POOL_WINDOWS)


def _pool_kernel(u_ref, w_ref, scale_ref, o_ref, ext_scr, *, steps, nbatch):
    c = pl.program_id(0)
    R = steps * nbatch
    H = POOL_HALO_STEPS * nbatch

    @pl.when(c == 0)
    def _():
        ext_scr[0:H, :] = jnp.zeros((H, POOL_WIDTH), F32)

    @pl.when(c > 0)
    def _():
        ext_scr[0:H, :] = ext_scr[R:R + H, :]

    ext_scr[H:H + R, :] = u_ref[...].astype(F32)

    t_glob = c * steps + lax.broadcasted_iota(jnp.int32, (R, 1), 0) // nbatch
    outs = []
    for gi, w in enumerate(POOL_WINDOWS):
        cols = slice(gi * POOL_GROUP, (gi + 1) * POOL_GROUP)
        u = ext_scr[H:H + R, cols]
        acc = u
        for k in range(1, w):
            acc = acc + ext_scr[H - k * nbatch:H - k * nbatch + R, cols]
        count = jnp.minimum(t_glob + 1, w).astype(F32)
        d = acc / count - u
        outs.append(jnp.dot(d.astype(BF16), w_ref[gi], preferred_element_type=F32))
    y = jnp.concatenate(outs, axis=1) * scale_ref[...]
    o_ref[...] = y.astype(o_ref.dtype)


def _pool(z, w_pool, pool_scale, nbatch):
    T = z.shape[0]
    steps = 64
    R = steps * nbatch
    H = POOL_HALO_STEPS * nbatch
    kern = functools.partial(_pool_kernel, steps=steps, nbatch=nbatch)
    return pl.pallas_call(
        kern,
        grid=(T // R,),
        in_specs=[
            pl.BlockSpec((R, 512), lambda c: (c, COL_POOL)),
            pl.BlockSpec((len(POOL_WINDOWS), POOL_GROUP, POOL_GROUP), lambda c: (0, 0, 0)),
            pl.BlockSpec((1, POOL_WIDTH), lambda c: (0, 0)),
        ],
        out_specs=pl.BlockSpec((R, POOL_WIDTH), lambda c: (c, 0)),
        out_shape=jax.ShapeDtypeStruct((T, POOL_WIDTH), BF16),
        scratch_shapes=[pltpu.VMEM((H + R, POOL_WIDTH), F32)],
        compiler_params=_cparams("arbitrary"),
        name="pool_mixer",
    )(z, w_pool, pool_scale)


def _s5_prep_kernel(ar_ref, ai_ref, ldt_ref, br_ref, bi_ref, lr_ref, li_ref, bbr_ref, bbi_ref):
    ar, ai = ar_ref[...], ai_ref[...]
    dt = jnp.exp(ldt_ref[...])
    mag = jnp.exp(ar * dt)
    lr = mag * jnp.cos(ai * dt)
    li = mag * jnp.sin(ai * dt)
    den = ar * ar + ai * ai
    nr = lr - 1.0
    fr = (nr * ar + li * ai) / den
    fi = (li * ar - nr * ai) / den
    br, bi = br_ref[...], bi_ref[...]
    lr_ref[...] = lr
    li_ref[...] = li
    bbr_ref[...] = fr * br - fi * bi
    bbi_ref[...] = fr * bi + fi * br


def _s5_prep(a_re, a_im, log_dt, b_re, b_im):
    L, G, N = a_re.shape
    P = b_re.shape[-1]
    rows, cols = L * G, N * P
    exp_n = lambda a: jnp.broadcast_to(a[..., None], (L, G, N, P)).reshape(rows, cols)
    ldt = jnp.broadcast_to(log_dt[..., None, None], (L, G, N, P)).reshape(rows, cols)
    shp = jax.ShapeDtypeStruct((rows, cols), F32)
    spec = pl.BlockSpec((rows, cols), lambda: (0, 0))
    lr, li, bbr, bbi = pl.pallas_call(
        _s5_prep_kernel,
        in_specs=[spec] * 5,
        out_specs=[spec] * 4,
        out_shape=[shp] * 4,
        name="s5_discretise",
    )(exp_n(a_re), exp_n(a_im), ldt, b_re.reshape(rows, cols), b_im.reshape(rows, cols))
    lr = lr.reshape(L, G, N, P)[..., 0].reshape(L, G * N)
    li = li.reshape(L, G, N, P)[..., 0].reshape(L, G * N)
    return lr, li, bbr.reshape(L, G, N, P), bbi.reshape(L, G, N, P)


S5_GPB = LANES // SSM_GROUP
S5_NBLK = SSM_WIDTH // LANES
S5_SB = S5_GPB * SSM_STATE
S5_NS = SSM_GROUPS * SSM_STATE


def _s5_layout(lr, li, bbr, bbi, c_re, c_im, nbatch):
    eye = jnp.eye(S5_GPB, dtype=F32)

    def bmat(bb):
        bb = bb.reshape(S5_NBLK, S5_GPB, SSM_STATE, SSM_GROUP)
        m = jnp.einsum('jgnp,gh->jgphn', bb, eye)
        return m.reshape(S5_NBLK, LANES, S5_SB)

    def cmat(c):
        c = c.reshape(S5_NBLK, S5_GPB, SSM_GROUP, SSM_STATE)
        m = jnp.einsum('jgpn,gh->jgnhp', c, eye)
        return m.reshape(S5_NBLK, S5_SB, LANES)

    b_mat = jnp.concatenate([bmat(bbr), bmat(bbi)], axis=2).astype(BF16)
    c_mat = jnp.concatenate([cmat(c_re), -cmat(c_im)], axis=1).astype(BF16)
    lam = jnp.concatenate([jnp.broadcast_to(lr[None, :], (nbatch, S5_NS)),
                           jnp.broadcast_to(li[None, :], (nbatch, S5_NS))], axis=0)
    return b_mat, c_mat, lam


S5_SCAN_COLS = 1024


def _s5_kernel(u_ref, bmat_ref, lam_ref, cmat_ref, d_ref, wglu_ref, bglu_ref, o_ref,
               x_scr, h_scr, *, steps, nbatch):
    @pl.when(pl.program_id(0) == 0)
    def _():
        h_scr[...] = jnp.zeros_like(h_scr)

    u = u_ref[...]
    for j in range(S5_NBLK):
        xj = jnp.dot(u[:, j * LANES:(j + 1) * LANES], bmat_ref[j], preferred_element_type=F32)
        x_scr[:, j * S5_SB:(j + 1) * S5_SB] = xj[:, :S5_SB]
        x_scr[:, S5_NS + j * S5_SB:S5_NS + (j + 1) * S5_SB] = xj[:, S5_SB:]

    for c in range(S5_NS // S5_SCAN_COLS):
        re = slice(c * S5_SCAN_COLS, (c + 1) * S5_SCAN_COLS)
        im = slice(S5_NS + c * S5_SCAN_COLS, S5_NS + (c + 1) * S5_SCAN_COLS)
        lr = lam_ref[0:nbatch, re]
        li = lam_ref[nbatch:2 * nbatch, re]

        def body(t, carry, re=re, im=im, lr=lr, li=li):
            hr, hi = carry
            r0 = pl.multiple_of(t * nbatch, nbatch)
            nhr = lr * hr - li * hi + x_scr[pl.ds(r0, nbatch), re]
            nhi = lr * hi + li * hr + x_scr[pl.ds(r0, nbatch), im]
            x_scr[pl.ds(r0, nbatch), re] = nhr
            x_scr[pl.ds(r0, nbatch), im] = nhi
            return nhr, nhi

        hr, hi = lax.fori_loop(0, steps, body, (h_scr[:, re], h_scr[:, im]), unroll=4)
        h_scr[:, re] = hr
        h_scr[:, im] = hi

    ys = []
    for j in range(S5_NBLK):
        hr = x_scr[:, j * S5_SB:(j + 1) * S5_SB].astype(BF16)
        hi = x_scr[:, S5_NS + j * S5_SB:S5_NS + (j + 1) * S5_SB].astype(BF16)
        ys.append(jnp.dot(hr, cmat_ref[j, 0:S5_SB, :], preferred_element_type=F32)
                  + jnp.dot(hi, cmat_ref[j, S5_SB:2 * S5_SB, :], preferred_element_type=F32))
    y = jnp.concatenate(ys, axis=1) + d_ref[...] * u.astype(F32)
    y = jax.nn.gelu(y, approximate=True)
    z = jnp.dot(y.astype(BF16), wglu_ref[...], preferred_element_type=F32) + bglu_ref[...]
    o_ref[...] = (z[:, :SSM_WIDTH] * jax.nn.sigmoid(z[:, SSM_WIDTH:])).astype(o_ref.dtype)


def _s5(z, b_mat, lam, c_mat, d_skip, w_glu, b_glu, nbatch):
    T = z.shape[0]
    steps = 32
    R = steps * nbatch
    kern = functools.partial(_s5_kernel, steps=steps, nbatch=nbatch)
    const = lambda *shape: pl.BlockSpec(shape, lambda c: (0,) * len(shape))
    return pl.pallas_call(
        kern,
        grid=(T // R,),
        in_specs=[
            pl.BlockSpec((R, 512), lambda c: (c, COL_SSM)),
            const(S5_NBLK, LANES, 2 * S5_SB),
            const(2 * nbatch, S5_NS),
            const(S5_NBLK, 2 * S5_SB, LANES),
            const(1, SSM_WIDTH),
            const(SSM_WIDTH, 2 * SSM_WIDTH),
            const(1, 2 * SSM_WIDTH),
        ],
        out_specs=pl.BlockSpec((R, SSM_WIDTH), lambda c: (c, 0)),
        out_shape=jax.ShapeDtypeStruct((T, SSM_WIDTH), BF16),
        scratch_shapes=[pltpu.VMEM((R, 2 * S5_NS), F32), pltpu.VMEM((nbatch, 2 * S5_NS), F32)],
        compiler_params=_cparams("arbitrary"),
        name="s5_mixer",
    )(z, b_mat, lam, c_mat, d_skip, w_glu, b_glu)


def _merge_kernel(h_ref, ya_ref, yb_ref, yc_ref, g0a, g0b, g1a, g1b, g2a, g2b,
                  wbr_ref, wout_ref, lng_ref, lnb_ref, o_ref):
    ys = (ya_ref, yb_ref, yc_ref)
    gs = ((g0a, g0b), (g1a, g1b), (g2a, g2b))
    half = D_MODEL // 2
    merged = []
    for hcol in range(2):
        acc = None
        for n in range(N_BRANCH):
            proj = jnp.dot(ys[n][...], wbr_ref[n, :, hcol * half:(hcol + 1) * half],
                           preferred_element_type=F32)
            term = jax.nn.sigmoid(gs[n][hcol][...].astype(F32)) * proj
            acc = term if acc is None else acc + term
        merged.append(acc.astype(BF16))
    merged = jnp.concatenate(merged, axis=1)
    mix = jnp.dot(merged, wout_ref[...], preferred_element_type=F32)
    r = DN_ALPHA * h_ref[...] + mix
    o_ref[...] = _layer_norm(r, lng_ref[...], lnb_ref[...])


def _merge(h, y_a, y_b, y_c, z, w_branch, w_out, ln_g, ln_b):
    T = h.shape[0]
    tm = 512
    row = lambda w: pl.BlockSpec((tm, w), lambda i: (i, 0))
    gate = lambda k: pl.BlockSpec((tm, 512), lambda i, k=k: (i, COL_GATE + k))
    return pl.pallas_call(
        _merge_kernel,
        grid=(T // tm,),
        in_specs=[row(D_MODEL), row(512), row(512), row(512)]
        + [gate(k) for k in range(2 * N_BRANCH)]
        + [
            pl.BlockSpec((N_BRANCH, BRANCH_WIDTH, D_MODEL), lambda i: (0, 0, 0)),
            pl.BlockSpec((D_MODEL, D_MODEL), lambda i: (0, 0)),
            pl.BlockSpec((1, D_MODEL), lambda i: (0, 0)),
            pl.BlockSpec((1, D_MODEL), lambda i: (0, 0)),
        ],
        out_specs=row(D_MODEL),
        out_shape=jax.ShapeDtypeStruct((T, D_MODEL), F32),
        compiler_params=_cparams("arbitrary"),
        name="merge_ln",
    )(h, y_a, y_b, y_c, z, z, z, z, z, z, w_branch, w_out, ln_g, ln_b)


def _ffn_kernel(x_ref, w1_ref, w3_ref, w2_ref, lng_ref, lnb_ref, o_ref, xb_ref, acc_ref):
    f = pl.program_id(1)

    @pl.when(f == 0)
    def _():
        xb_ref[...] = x_ref[...].astype(BF16)
        acc_ref[...] = jnp.zeros_like(acc_ref)

    xb = xb_ref[...]
    a = jax.nn.silu(jnp.dot(xb, w1_ref[...], preferred_element_type=F32))
    a = a * jnp.dot(xb, w3_ref[...], preferred_element_type=F32)
    acc_ref[...] += jnp.dot(a.astype(BF16), w2_ref[...], preferred_element_type=F32)

    @pl.when(f == pl.num_programs(1) - 1)
    def _():
        r = DN_ALPHA * x_ref[...] + acc_ref[...]
        o_ref[...] = _layer_norm(r, lng_ref[...], lnb_ref[...])


FFN_TF = 1408


def _ffn(x, w1, w3, w2, ln_g, ln_b):
    T = x.shape[0]
    tm, tf = 512, FFN_TF
    return pl.pallas_call(
        _ffn_kernel,
        grid=(T // tm, D_FF // tf),
        in_specs=[
            pl.BlockSpec((tm, D_MODEL), lambda i, f: (i, 0)),
            pl.BlockSpec((D_MODEL, tf), lambda i, f: (0, f)),
            pl.BlockSpec((D_MODEL, tf), lambda i, f: (0, f)),
            pl.BlockSpec((tf, D_MODEL), lambda i, f: (f, 0)),
            pl.BlockSpec((1, D_MODEL), lambda i, f: (0, 0)),
            pl.BlockSpec((1, D_MODEL), lambda i, f: (0, 0)),
        ],
        out_specs=pl.BlockSpec((tm, D_MODEL), lambda i, f: (i, 0)),
        out_shape=jax.ShapeDtypeStruct((T, D_MODEL), F32),
        scratch_shapes=[pltpu.VMEM((tm, D_MODEL), BF16), pltpu.VMEM((tm, D_MODEL), F32)],
        compiler_params=_cparams("arbitrary", "arbitrary"),
        name="ffn_ln",
    )(x, w1, w3, w2, ln_g, ln_b)


def _router_kernel(x_ref, w_ref, b_ref, o_ref):
    logits = jnp.dot(x_ref[...], w_ref[...], preferred_element_type=F32,
                     precision=lax.Precision.HIGHEST) + b_ref[...]
    lane = lax.broadcasted_iota(jnp.int32, logits.shape, 1)
    l1 = jnp.where(lane < N_EXPERTS, logits, -jnp.inf)
    m1 = jnp.max(l1, axis=1, keepdims=True)
    i1 = jnp.min(jnp.where(l1 == m1, lane, LANES), axis=1, keepdims=True)
    l2 = jnp.where(lane == i1, -jnp.inf, l1)
    m2 = jnp.max(l2, axis=1, keepdims=True)
    i2 = jnp.min(jnp.where(l2 == m2, lane, LANES), axis=1, keepdims=True)
    e = jnp.exp(m2 - m1)
    w_top = 1.0 / (1.0 + e)
    w_sec = e / (1.0 + e)
    o_ref[...] = jnp.where(lane == i1, w_top, 0.0) + jnp.where(lane == i2, w_sec, 0.0)


def _router(x, w, b):
    T = x.shape[0]
    tm = 1024
    return pl.pallas_call(
        _router_kernel,
        grid=(T // tm,),
        in_specs=[
            pl.BlockSpec((tm, D_MODEL), lambda i: (i, 0)),
            pl.BlockSpec((D_MODEL, LANES), lambda i: (0, 0)),
            pl.BlockSpec((1, LANES), lambda i: (0, 0)),
        ],
        out_specs=pl.BlockSpec((tm, LANES), lambda i: (i, 0)),
        out_shape=jax.ShapeDtypeStruct((T, LANES), F32),
        compiler_params=_cparams("arbitrary"),
        name="moe_router",
    )(x, w, b)


def _moe_kernel(x_ref, gate_ref, w1_ref, w3_ref, w2_ref, lng_ref, lnb_ref, o_ref, xb_ref, acc_ref):
    e = pl.program_id(1)
    f = pl.program_id(2)

    @pl.when((e == 0) & (f == 0))
    def _():
        xb_ref[...] = x_ref[...].astype(BF16)
        acc_ref[...] = jnp.zeros_like(acc_ref)

    lane = lax.broadcasted_iota(jnp.int32, gate_ref.shape, 1)
    g = jnp.sum(jnp.where(lane == e, gate_ref[...], 0.0), axis=1, keepdims=True)
    xb = xb_ref[...]
    a = jax.nn.silu(jnp.dot(xb, w1_ref[0], preferred_element_type=F32))
    a = a * jnp.dot(xb, w3_ref[0], preferred_element_type=F32)
    y = jnp.dot(a.astype(BF16), w2_ref[0], preferred_element_type=F32)
    acc_ref[...] += g * y

    @pl.when((e == pl.num_programs(1) - 1) & (f == pl.num_programs(2) - 1))
    def _():
        r = DN_ALPHA * x_ref[...] + acc_ref[...]
        o_ref[...] = _layer_norm(r, lng_ref[...], lnb_ref[...])


def _moe(x, gates, w1, w3, w2, ln_g, ln_b):
    T = x.shape[0]
    tm, tf = 512, FFN_TF
    return pl.pallas_call(
        _moe_kernel,
        grid=(T // tm, N_EXPERTS, D_FF // tf),
        in_specs=[
            pl.BlockSpec((tm, D_MODEL), lambda i, e, f: (i, 0)),
            pl.BlockSpec((tm, LANES), lambda i, e, f: (i, 0)),
            pl.BlockSpec((1, D_MODEL, tf), lambda i, e, f: (e, 0, f)),
            pl.BlockSpec((1, D_MODEL, tf), lambda i, e, f: (e, 0, f)),
            pl.BlockSpec((1, tf, D_MODEL), lambda i, e, f: (e, f, 0)),
            pl.BlockSpec((1, D_MODEL), lambda i, e, f: (0, 0)),
            pl.BlockSpec((1, D_MODEL), lambda i, e, f: (0, 0)),
        ],
        out_specs=pl.BlockSpec((tm, D_MODEL), lambda i, e, f: (i, 0)),
        out_shape=jax.ShapeDtypeStruct((T, D_MODEL), F32),
        scratch_shapes=[pltpu.VMEM((tm, D_MODEL), BF16), pltpu.VMEM((tm, D_MODEL), F32)],
        compiler_params=_cparams("arbitrary", "arbitrary", "arbitrary"),
        name="moe_ln",
    )(x, gates, w1, w3, w2, ln_g, ln_b)


def kernel(x, ln1_g, ln1_b, w_in, b_in, w_pool, pool_scale, ssm_a_re, ssm_a_im, ssm_log_dt,
           ssm_b_re, ssm_b_im, ssm_c_re, ssm_c_im, ssm_d, w_glu, b_glu, w_branch, w_out,
           ln2_g, ln2_b, ffn_w1, ffn_w3, ffn_w2, moe_router, moe_router_b, moe_w1, moe_w3, moe_w2):
    B, S, D = x.shape
    T = B * S
    assert B == SUBLANES and D == D_MODEL
    h = x.transpose(1, 0, 2).reshape(T, D)

    lr, li, bbr, bbi = _s5_prep(ssm_a_re, ssm_a_im, ssm_log_dt, ssm_b_re, ssm_b_im)
    row = lambda a: a.reshape(1, -1)

    for i in range(DEPTH):
        z = _inproj(h, w_in[i].astype(BF16), row(b_in[i]))
        y_b = _attention(z.reshape(S, B * IN_WIDTH), B, S).reshape(T, ATT_WIDTH)
        y_a = _pool(z, w_pool[i].astype(BF16), row(pool_scale[i]), B)
        b_mat, c_mat, lam = _s5_layout(lr[i], li[i], bbr[i], bbi[i], ssm_c_re[i], ssm_c_im[i], B)
        y_c = _s5(z, b_mat, lam, c_mat, row(ssm_d[i]), w_glu[i].astype(BF16), row(b_glu[i]), B)
        h = _merge(h, y_a, y_b, y_c, z, w_branch[i].astype(BF16), w_out[i].astype(BF16),
                   row(ln1_g[i]), row(ln1_b[i]))
        j = i // 2
        if i % 2 == 0:
            h = _ffn(h, ffn_w1[j].astype(BF16), ffn_w3[j].astype(BF16), ffn_w2[j].astype(BF16),
                     row(ln2_g[i]), row(ln2_b[i]))
        else:
            rw = jnp.pad(moe_router[j], ((0, 0), (0, LANES - N_EXPERTS)))
            rb = jnp.pad(moe_router_b[j], (0, LANES - N_EXPERTS)).reshape(1, LANES)
            gates = _router(h, rw, rb)
            h = _moe(h, gates, moe_w1[j].astype(BF16), moe_w3[j].astype(BF16),
                     moe_w2[j].astype(BF16), row(ln2_g[i]), row(ln2_b[i]))
    return h.reshape(S, B, D).transpose(1, 0, 2)
```

```python
import functools

import jax
import jax.numpy as jnp
from jax import lax
from jax.experimental import pallas as pl
from jax.experimental.pallas import tpu as pltpu

F32 = jnp.float32
BF16 = jnp.bfloat16

D_MODEL = 1024
DEPTH = 4
POOL_WINDOWS = (2, 4, 8, 16)
POOL_WIDTH = 512
POOL_GROUP = POOL_WIDTH // len(POOL_WINDOWS)
ATT_HEADS = 8
HEAD_DIM = 64
ATT_WIDTH = ATT_HEADS * HEAD_DIM
MOBA_BLOCK = 256
MOBA_TOPK = 3
SSM_WIDTH = 512
SSM_GROUP = 16
SSM_GROUPS = SSM_WIDTH // SSM_GROUP
SSM_STATE = 64
N_BRANCH = 3
BRANCH_WIDTH = 512
IN_WIDTH = POOL_WIDTH + 3 * ATT_WIDTH + SSM_WIDTH + N_BRANCH * D_MODEL
D_FF = 2816
N_EXPERTS = 8
TOP_K = 2
DN_ALPHA = (2 * DEPTH) ** 0.25
LN_EPS = 1e-5

LANES = 128
SUBLANES = 8
VMEM_LIMIT = 48 * 1024 * 1024

NEG = -1e30

COL_POOL, COL_SSM, COL_GATE = 0, 1, 2
QKV_WIDTH = 3 * ATT_WIDTH
Z_WIDTH = IN_WIDTH - QKV_WIDTH
N_QKV_BLOCKS = QKV_WIDTH // 512


def _cparams(*sem):
    return pltpu.CompilerParams(dimension_semantics=sem, vmem_limit_bytes=VMEM_LIMIT)


def _layer_norm(r, g, b):
    mu = jnp.mean(r, axis=-1, keepdims=True)
    c = r - mu
    var = jnp.mean(c * c, axis=-1, keepdims=True)
    return c * lax.rsqrt(var + LN_EPS) * g + b


def _inproj_kernel(x_ref, w_ref, b_ref, qkv_ref, z_ref, xb_ref, acc_scr, *, nbatch):
    j = pl.program_id(1)

    @pl.when(j == 0)
    def _():
        xb_ref[...] = x_ref[...].astype(BF16)

    acc = jnp.dot(xb_ref[...], w_ref[...], preferred_element_type=F32) + b_ref[...]

    @pl.when(j < N_QKV_BLOCKS)
    def _():
        nslab, rows, _ = acc_scr.shape
        steps = rows // nbatch
        for c in range(nslab):
            acc_scr[c] = acc[:, c * LANES:(c + 1) * LANES]
        for b in range(nbatch):
            for c in range(nslab):
                qkv_ref[b, :, c * LANES:(c + 1) * LANES] = (
                    acc_scr[c, pl.ds(b, steps, stride=nbatch), :].astype(qkv_ref.dtype))

    @pl.when(j >= N_QKV_BLOCKS)
    def _():
        z_ref[...] = acc.astype(z_ref.dtype)


def _inproj(x, w, b, nbatch):
    T, K = x.shape
    N = w.shape[1]
    S = T // nbatch
    tm, tn = 1024, 512
    steps = tm // nbatch
    kern = functools.partial(_inproj_kernel, nbatch=nbatch)
    return pl.pallas_call(
        kern,
        grid=(T // tm, N // tn),
        in_specs=[
            pl.BlockSpec((tm, K), lambda i, j: (i, 0)),
            pl.BlockSpec((K, tn), lambda i, j: (0, j)),
            pl.BlockSpec((1, tn), lambda i, j: (0, j)),
        ],
        out_specs=[
            pl.BlockSpec((nbatch, steps, tn), lambda i, j: (0, i, jnp.minimum(j, N_QKV_BLOCKS - 1))),
            pl.BlockSpec((tm, tn), lambda i, j: (i, jnp.maximum(j - N_QKV_BLOCKS, 0))),
        ],
        out_shape=[jax.ShapeDtypeStruct((nbatch, S, QKV_WIDTH), BF16),
                   jax.ShapeDtypeStruct((T, Z_WIDTH), BF16)],
        scratch_shapes=[pltpu.VMEM((tm, K), BF16), pltpu.VMEM((tn // LANES, tm, LANES), F32)],
        compiler_params=_cparams("arbitrary", "arbitrary"),
        name="inproj",
    )(x, w, b)


def _attn_block(i, q_ref, k_ref, v_ref, o_ref, kmh_ref, kml_ref, s_scr, p_scr,
                *, nb, blk, dh, n_heads, topk):
    pair = pl.program_id(1)
    W = 2 * dh
    lane = lax.broadcasted_iota(jnp.int32, (1, W), 1)
    head_of_lane = lane // dh
    nt = (((1,), (1,)), ((), ()))

    if i == 0:
        kmh_ref[...] = jnp.zeros_like(kmh_ref)
        kml_ref[...] = jnp.zeros_like(kml_ref)
        for n in range(nb):
            kb = k_ref[n * blk:(n + 1) * blk, :].astype(F32)
            mean = jnp.sum(kb, axis=0, keepdims=True) * (1.0 / blk)
            for hl in range(2):
                row = jnp.where(head_of_lane == hl, mean, 0.0)
                hi = row.astype(BF16)
                lo = (row - hi.astype(F32)).astype(BF16)
                kmh_ref[hl * nb + n:hl * nb + n + 1, :] = hi
                kml_ref[hl * nb + n:hl * nb + n + 1, :] = lo

    q = q_ref[...] * jnp.asarray(dh ** -0.5, BF16)
    slopes = []
    for hl in range(2):
        h_f = (pair * 2 + hl + 1).astype(F32)
        slopes.append(jnp.exp2(jnp.full((1, blk), -8.0 / n_heads, F32) * h_f))

    if i > 0:
        gate_t = (lax.dot_general(kmh_ref[...], q, nt, preferred_element_type=F32)
                  + lax.dot_general(kml_ref[...], q, nt, preferred_element_type=F32))
        row = lax.broadcasted_iota(jnp.int32, (nb, blk), 0)
        rowdist = ((row - i) * blk).astype(F32)
        bias_parts = []
        for hl in range(2):
            g = gate_t[hl * nb:(hl + 1) * nb, :]
            cnt = jnp.zeros((nb, blk), F32)
            for n2 in range(i):
                r = g[n2:n2 + 1, :]
                cnt = cnt + jnp.where(r > g, 1.0,
                                      jnp.where(r == g, jnp.where(n2 < row, 1.0, 0.0), 0.0))
            sel = jnp.where(row < i, jnp.where(cnt < float(topk), 1.0, 0.0), 0.0)
            bias_parts.append(jnp.where(sel > 0.5, slopes[hl] * rowdist, NEG))
        bias_t = jnp.concatenate(bias_parts + [jnp.zeros((LANES - 2 * nb, blk), F32)], axis=0)
        bias = bias_t.T

    key_pos = lax.broadcasted_iota(jnp.int32, (1, blk), 1).astype(F32)
    r_io = lax.broadcasted_iota(jnp.int32, (blk, blk), 0)
    c_io = lax.broadcasted_iota(jnp.int32, (blk, blk), 1)
    nkeys = (i + 1) * blk

    outs = []
    for hl in range(2):
        qh = jnp.where(head_of_lane == hl, q, jnp.zeros_like(q))
        keybias = slopes[hl] * key_pos
        macc = None
        for n in range(i + 1):
            s = lax.dot_general(qh, k_ref[n * blk:(n + 1) * blk, :], nt,
                                preferred_element_type=F32) + keybias
            if n == i:
                s = jnp.where(c_io <= r_io, s, NEG)
            else:
                c = hl * nb + n
                s = s + bias[:, c:c + 1]
            s_scr[hl, :, n * blk:(n + 1) * blk] = s
            t = jnp.maximum(s[:, :LANES], s[:, LANES:])
            macc = t if macc is None else jnp.maximum(macc, t)
        m = jnp.max(macc, axis=1, keepdims=True)
        lacc = None
        for n in range(i + 1):
            p = jnp.exp(s_scr[hl, :, n * blk:(n + 1) * blk] - m)
            p_scr[hl, :, n * blk:(n + 1) * blk] = p.astype(BF16)
            t = p[:, :LANES] + p[:, LANES:]
            lacc = t if lacc is None else lacc + t
        l = jnp.sum(lacc, axis=1, keepdims=True)
        acc = jnp.dot(p_scr[hl, :, 0:nkeys], v_ref[0:nkeys, :], preferred_element_type=F32)
        outs.append(acc * (1.0 / l))

    o_ref[...] = jnp.where(head_of_lane == 0, outs[0], outs[1]).astype(o_ref.dtype)


def _attn_kernel(*refs, nb, **kw):
    i = pl.program_id(2)
    for k in range(nb):
        pl.when(i == k)(functools.partial(_attn_block, k, *refs, nb=nb, **kw))


def _attention(qkv):
    B, S, _ = qkv.shape
    blk, dh = MOBA_BLOCK, HEAD_DIM
    nb = S // blk
    assert nb * blk == S and nb == SUBLANES and 2 * dh == LANES
    n_pairs = ATT_HEADS // 2
    kern = functools.partial(_attn_kernel, nb=nb, blk=blk, dh=dh, n_heads=ATT_HEADS, topk=MOBA_TOPK)
    return pl.pallas_call(
        kern,
        grid=(B, n_pairs, nb),
        in_specs=[
            pl.BlockSpec((None, blk, LANES), lambda b, p, i: (b, i, p)),
            pl.BlockSpec((None, S, LANES), lambda b, p, i: (b, 0, n_pairs + p)),
            pl.BlockSpec((None, S, LANES), lambda b, p, i: (b, 0, 2 * n_pairs + p)),
        ],
        out_specs=pl.BlockSpec((None, blk, LANES), lambda b, p, i: (b, i, p)),
        out_shape=jax.ShapeDtypeStruct((B, S, ATT_WIDTH), F32),
        scratch_shapes=[
            pltpu.VMEM((LANES, LANES), BF16),
            pltpu.VMEM((LANES, LANES), BF16),
            pltpu.VMEM((2, blk, nb * blk), F32),
            pltpu.VMEM((2, blk, nb * blk), BF16),
        ],
        compiler_params=_cparams("arbitrary", "arbitrary", "arbitrary"),
        name="moba_attn",
    )(qkv, qkv, qkv)


POOL_HALO_STEPS = max(POOL_WINDOWS)


def _pool_kernel(u_ref, w_ref, scale_ref, o_ref, ext_scr, *, steps, nbatch):
    c = pl.program_id(0)
    R = steps * nbatch
    H = POOL_HALO_STEPS * nbatch

    @pl.when(c == 0)
    def _():
        ext_scr[0:H, :] = jnp.zeros((H, POOL_WIDTH), F32)

    @pl.when(c > 0)
    def _():
        ext_scr[0:H, :] = ext_scr[R:R + H, :]

    ext_scr[H:H + R, :] = u_ref[...].astype(F32)

    t_glob = c * steps + lax.broadcasted_iota(jnp.int32, (R, 1), 0) // nbatch
    outs = []
    for gi, w in enumerate(POOL_WINDOWS):
        cols = slice(gi * POOL_GROUP, (gi + 1) * POOL_GROUP)
        u = ext_scr[H:H + R, cols]
        acc = u
        for k in range(1, w):
            acc = acc + ext_scr[H - k * nbatch:H - k * nbatch + R, cols]
        count = jnp.minimum(t_glob + 1, w).astype(F32)
        d = acc / count - u
        outs.append(jnp.dot(d.astype(BF16), w_ref[gi], preferred_element_type=F32))
    y = jnp.concatenate(outs, axis=1) * scale_ref[...]
    o_ref[...] = y.astype(o_ref.dtype)


def _pool(z, w_pool, pool_scale, nbatch):
    T = z.shape[0]
    steps = 64
    R = steps * nbatch
    H = POOL_HALO_STEPS * nbatch
    kern = functools.partial(_pool_kernel, steps=steps, nbatch=nbatch)
    return pl.pallas_call(
        kern,
        grid=(T // R,),
        in_specs=[
            pl.BlockSpec((R, 512), lambda c: (c, COL_POOL)),
            pl.BlockSpec((len(POOL_WINDOWS), POOL_GROUP, POOL_GROUP), lambda c: (0, 0, 0)),
            pl.BlockSpec((1, POOL_WIDTH), lambda c: (0, 0)),
        ],
        out_specs=pl.BlockSpec((R, POOL_WIDTH), lambda c: (c, 0)),
        out_shape=jax.ShapeDtypeStruct((T, POOL_WIDTH), BF16),
        scratch_shapes=[pltpu.VMEM((H + R, POOL_WIDTH), F32)],
        compiler_params=_cparams("arbitrary"),
        name="pool_mixer",
    )(z, w_pool, pool_scale)


def _s5_prep_kernel(ar_ref, ai_ref, ldt_ref, br_ref, bi_ref, lr_ref, li_ref, bbr_ref, bbi_ref):
    ar, ai = ar_ref[...], ai_ref[...]
    dt = jnp.exp(ldt_ref[...])
    mag = jnp.exp(ar * dt)
    lr = mag * jnp.cos(ai * dt)
    li = mag * jnp.sin(ai * dt)
    den = ar * ar + ai * ai
    nr = lr - 1.0
    fr = (nr * ar + li * ai) / den
    fi = (li * ar - nr * ai) / den
    br, bi = br_ref[...], bi_ref[...]
    lr_ref[...] = lr
    li_ref[...] = li
    bbr_ref[...] = fr * br - fi * bi
    bbi_ref[...] = fr * bi + fi * br


def _s5_prep(a_re, a_im, log_dt, b_re, b_im):
    L, G, N = a_re.shape
    P = b_re.shape[-1]
    rows, cols = L * G, N * P
    exp_n = lambda a: jnp.broadcast_to(a[..., None], (L, G, N, P)).reshape(rows, cols)
    ldt = jnp.broadcast_to(log_dt[..., None, None], (L, G, N, P)).reshape(rows, cols)
    shp = jax.ShapeDtypeStruct((rows, cols), F32)
    spec = pl.BlockSpec((rows, cols), lambda: (0, 0))
    lr, li, bbr, bbi = pl.pallas_call(
        _s5_prep_kernel,
        in_specs=[spec] * 5,
        out_specs=[spec] * 4,
        out_shape=[shp] * 4,
        name="s5_discretise",
    )(exp_n(a_re), exp_n(a_im), ldt, b_re.reshape(rows, cols), b_im.reshape(rows, cols))
    lr = lr.reshape(L, G, N, P)[..., 0].reshape(L, G * N)
    li = li.reshape(L, G, N, P)[..., 0].reshape(L, G * N)
    return lr, li, bbr.reshape(L, G, N, P), bbi.reshape(L, G, N, P)


S5_GPB = LANES // SSM_GROUP
S5_NBLK = SSM_WIDTH // LANES
S5_SB = S5_GPB * SSM_STATE
S5_NS = SSM_GROUPS * SSM_STATE


def _s5_layout(lr, li, bbr, bbi, c_re, c_im, nbatch):
    eye = jnp.eye(S5_GPB, dtype=F32)

    def bmat(bb):
        bb = bb.reshape(S5_NBLK, S5_GPB, SSM_STATE, SSM_GROUP)
        m = jnp.einsum('jgnp,gh->jgphn', bb, eye)
        return m.reshape(S5_NBLK, LANES, S5_SB)

    def cmat(c):
        c = c.reshape(S5_NBLK, S5_GPB, SSM_GROUP, SSM_STATE)
        m = jnp.einsum('jgpn,gh->jgnhp', c, eye)
        return m.reshape(S5_NBLK, S5_SB, LANES)

    b_mat = jnp.concatenate([bmat(bbr), bmat(bbi)], axis=2).astype(BF16)
    c_mat = jnp.concatenate([cmat(c_re), -cmat(c_im)], axis=1).astype(BF16)
    lam = jnp.concatenate([jnp.broadcast_to(lr[None, :], (nbatch, S5_NS)),
                           jnp.broadcast_to(li[None, :], (nbatch, S5_NS))], axis=0)
    return b_mat, c_mat, lam


S5_SCAN_COLS = 1024


def _s5_kernel(u_ref, bmat_ref, lam_ref, cmat_ref, d_ref, wglu_ref, bglu_ref, o_ref,
               x_scr, h_scr, *, steps, nbatch):
    @pl.when(pl.program_id(0) == 0)
    def _():
        h_scr[...] = jnp.zeros_like(h_scr)

    u = u_ref[...]
    for j in range(S5_NBLK):
        xj = jnp.dot(u[:, j * LANES:(j + 1) * LANES], bmat_ref[j], preferred_element_type=F32)
        x_scr[:, j * S5_SB:(j + 1) * S5_SB] = xj[:, :S5_SB]
        x_scr[:, S5_NS + j * S5_SB:S5_NS + (j + 1) * S5_SB] = xj[:, S5_SB:]

    for c in range(S5_NS // S5_SCAN_COLS):
        re = slice(c * S5_SCAN_COLS, (c + 1) * S5_SCAN_COLS)
        im = slice(S5_NS + c * S5_SCAN_COLS, S5_NS + (c + 1) * S5_SCAN_COLS)
        lr = lam_ref[0:nbatch, re]
        li = lam_ref[nbatch:2 * nbatch, re]

        def body(t, carry, re=re, im=im, lr=lr, li=li):
            hr, hi = carry
            r0 = pl.multiple_of(t * nbatch, nbatch)
            nhr = lr * hr - li * hi + x_scr[pl.ds(r0, nbatch), re]
            nhi = lr * hi + li * hr + x_scr[pl.ds(r0, nbatch), im]
            x_scr[pl.ds(r0, nbatch), re] = nhr
            x_scr[pl.ds(r0, nbatch), im] = nhi
            return nhr, nhi

        hr, hi = lax.fori_loop(0, steps, body, (h_scr[:, re], h_scr[:, im]), unroll=4)
        h_scr[:, re] = hr
        h_scr[:, im] = hi

    ys = []
    for j in range(S5_NBLK):
        hr = x_scr[:, j * S5_SB:(j + 1) * S5_SB].astype(BF16)
        hi = x_scr[:, S5_NS + j * S5_SB:S5_NS + (j + 1) * S5_SB].astype(BF16)
        ys.append(jnp.dot(hr, cmat_ref[j, 0:S5_SB, :], preferred_element_type=F32)
                  + jnp.dot(hi, cmat_ref[j, S5_SB:2 * S5_SB, :], preferred_element_type=F32))
    y = jnp.concatenate(ys, axis=1) + d_ref[...] * u.astype(F32)
    y = jax.nn.gelu(y, approximate=True)
    z = jnp.dot(y.astype(BF16), wglu_ref[...], preferred_element_type=F32) + bglu_ref[...]
    o_ref[...] = (z[:, :SSM_WIDTH] * jax.nn.sigmoid(z[:, SSM_WIDTH:])).astype(o_ref.dtype)


def _s5(z, b_mat, lam, c_mat, d_skip, w_glu, b_glu, nbatch):
    T = z.shape[0]
    steps = 32
    R = steps * nbatch
    kern = functools.partial(_s5_kernel, steps=steps, nbatch=nbatch)
    const = lambda *shape: pl.BlockSpec(shape, lambda c: (0,) * len(shape))
    return pl.pallas_call(
        kern,
        grid=(T // R,),
        in_specs=[
            pl.BlockSpec((R, 512), lambda c: (c, COL_SSM)),
            const(S5_NBLK, LANES, 2 * S5_SB),
            const(2 * nbatch, S5_NS),
            const(S5_NBLK, 2 * S5_SB, LANES),
            const(1, SSM_WIDTH),
            const(SSM_WIDTH, 2 * SSM_WIDTH),
            const(1, 2 * SSM_WIDTH),
        ],
        out_specs=pl.BlockSpec((R, SSM_WIDTH), lambda c: (c, 0)),
        out_shape=jax.ShapeDtypeStruct((T, SSM_WIDTH), BF16),
        scratch_shapes=[pltpu.VMEM((R, 2 * S5_NS), F32), pltpu.VMEM((nbatch, 2 * S5_NS), F32)],
        compiler_params=_cparams("arbitrary"),
        name="s5_mixer",
    )(z, b_mat, lam, c_mat, d_skip, w_glu, b_glu)


def _merge_kernel(h_ref, ya_ref, yb_ref, yc_ref, g0a, g0b, g1a, g1b, g2a, g2b,
                  wbr_ref, wout_ref, lng_ref, lnb_ref, o_ref, yb_scr):
    nbatch, steps, _ = yb_ref.shape
    nslab = yb_scr.shape[0]
    for b in range(nbatch):
        for c in range(nslab):
            yb_scr[c, pl.ds(b, steps, stride=nbatch), :] = yb_ref[b, :, c * LANES:(c + 1) * LANES]
    yb = jnp.concatenate([yb_scr[c] for c in range(nslab)], axis=1).astype(BF16)
    ys = (ya_ref[...], yb, yc_ref[...])
    gs = ((g0a, g0b), (g1a, g1b), (g2a, g2b))
    half = D_MODEL // 2
    merged = []
    for hcol in range(2):
        acc = None
        for n in range(N_BRANCH):
            proj = jnp.dot(ys[n], wbr_ref[n, :, hcol * half:(hcol + 1) * half],
                           preferred_element_type=F32)
            term = jax.nn.sigmoid(gs[n][hcol][...].astype(F32)) * proj
            acc = term if acc is None else acc + term
        merged.append(acc.astype(BF16))
    merged = jnp.concatenate(merged, axis=1)
    mix = jnp.dot(merged, wout_ref[...], preferred_element_type=F32)
    r = DN_ALPHA * h_ref[...] + mix
    o_ref[...] = _layer_norm(r, lng_ref[...], lnb_ref[...])


def _merge(h, y_a, y_b, y_c, z, w_branch, w_out, ln_g, ln_b):
    T = h.shape[0]
    nbatch = y_b.shape[0]
    tm = 512
    steps = tm // nbatch
    row = lambda w: pl.BlockSpec((tm, w), lambda i: (i, 0))
    gate = lambda k: pl.BlockSpec((tm, 512), lambda i, k=k: (i, COL_GATE + k))
    return pl.pallas_call(
        _merge_kernel,
        grid=(T // tm,),
        in_specs=[row(D_MODEL), row(512),
                  pl.BlockSpec((nbatch, steps, ATT_WIDTH), lambda i: (0, i, 0)),
                  row(512)]
        + [gate(k) for k in range(2 * N_BRANCH)]
        + [
            pl.BlockSpec((N_BRANCH, BRANCH_WIDTH, D_MODEL), lambda i: (0, 0, 0)),
            pl.BlockSpec((D_MODEL, D_MODEL), lambda i: (0, 0)),
            pl.BlockSpec((1, D_MODEL), lambda i: (0, 0)),
            pl.BlockSpec((1, D_MODEL), lambda i: (0, 0)),
        ],
        out_specs=row(D_MODEL),
        out_shape=jax.ShapeDtypeStruct((T, D_MODEL), F32),
        scratch_shapes=[pltpu.VMEM((ATT_WIDTH // LANES, tm, LANES), F32)],
        compiler_params=_cparams("arbitrary"),
        name="merge_ln",
    )(h, y_a, y_b, y_c, z, z, z, z, z, z, w_branch, w_out, ln_g, ln_b)


def _ffn_kernel(x_ref, w1_ref, w3_ref, w2_ref, lng_ref, lnb_ref, o_ref, xb_ref, acc_ref):
    f = pl.program_id(1)

    @pl.when(f == 0)
    def _():
        xb_ref[...] = x_ref[...].astype(BF16)
        acc_ref[...] = jnp.zeros_like(acc_ref)

    xb = xb_ref[...]
    a = jax.nn.silu(jnp.dot(xb, w1_ref[...], preferred_element_type=F32))
    a = a * jnp.dot(xb, w3_ref[...], preferred_element_type=F32)
    acc_ref[...] += jnp.dot(a.astype(BF16), w2_ref[...], preferred_element_type=F32)

    @pl.when(f == pl.num_programs(1) - 1)
    def _():
        r = DN_ALPHA * x_ref[...] + acc_ref[...]
        o_ref[...] = _layer_norm(r, lng_ref[...], lnb_ref[...])


FFN_TF = 1408


def _ffn(x, w1, w3, w2, ln_g, ln_b):
    T = x.shape[0]
    tm, tf = 512, FFN_TF
    return pl.pallas_call(
        _ffn_kernel,
        grid=(T // tm, D_FF // tf),
        in_specs=[
            pl.BlockSpec((tm, D_MODEL), lambda i, f: (i, 0)),
            pl.BlockSpec((D_MODEL, tf), lambda i, f: (0, f)),
            pl.BlockSpec((D_MODEL, tf), lambda i, f: (0, f)),
            pl.BlockSpec((tf, D_MODEL), lambda i, f: (f, 0)),
            pl.BlockSpec((1, D_MODEL), lambda i, f: (0, 0)),
            pl.BlockSpec((1, D_MODEL), lambda i, f: (0, 0)),
        ],
        out_specs=pl.BlockSpec((tm, D_MODEL), lambda i, f: (i, 0)),
        out_shape=jax.ShapeDtypeStruct((T, D_MODEL), F32),
        scratch_shapes=[pltpu.VMEM((tm, D_MODEL), BF16), pltpu.VMEM((tm, D_MODEL), F32)],
        compiler_params=_cparams("arbitrary", "arbitrary"),
        name="ffn_ln",
    )(x, w1, w3, w2, ln_g, ln_b)


def _router_kernel(x_ref, w_ref, b_ref, o_ref):
    logits = jnp.dot(x_ref[...], w_ref[...], preferred_element_type=F32,
                     precision=lax.Precision.HIGHEST) + b_ref[...]
    lane = lax.broadcasted_iota(jnp.int32, logits.shape, 1)
    l1 = jnp.where(lane < N_EXPERTS, logits, -jnp.inf)
    m1 = jnp.max(l1, axis=1, keepdims=True)
    i1 = jnp.min(jnp.where(l1 == m1, lane, LANES), axis=1, keepdims=True)
    l2 = jnp.where(lane == i1, -jnp.inf, l1)
    m2 = jnp.max(l2, axis=1, keepdims=True)
    i2 = jnp.min(jnp.where(l2 == m2, lane, LANES), axis=1, keepdims=True)
    e = jnp.exp(m2 - m1)
    w_top = 1.0 / (1.0 + e)
    w_sec = e / (1.0 + e)
    o_ref[...] = jnp.where(lane == i1, w_top, 0.0) + jnp.where(lane == i2, w_sec, 0.0)


def _router(x, w, b):
    T = x.shape[0]
    tm = 1024
    return pl.pallas_call(
        _router_kernel,
        grid=(T // tm,),
        in_specs=[
            pl.BlockSpec((tm, D_MODEL), lambda i: (i, 0)),
            pl.BlockSpec((D_MODEL, LANES), lambda i: (0, 0)),
            pl.BlockSpec((1, LANES), lambda i: (0, 0)),
        ],
        out_specs=pl.BlockSpec((tm, LANES), lambda i: (i, 0)),
        out_shape=jax.ShapeDtypeStruct((T, LANES), F32),
        compiler_params=_cparams("arbitrary"),
        name="moe_router",
    )(x, w, b)


def _moe_kernel(x_ref, gate_ref, w1_ref, w3_ref, w2_ref, lng_ref, lnb_ref, o_ref, xb_ref, acc_ref):
    e = pl.program_id(1)
    f = pl.program_id(2)

    @pl.when((e == 0) & (f == 0))
    def _():
        xb_ref[...] = x_ref[...].astype(BF16)
        acc_ref[...] = jnp.zeros_like(acc_ref)

    lane = lax.broadcasted_iota(jnp.int32, gate_ref.shape, 1)
    g = jnp.sum(jnp.where(lane == e, gate_ref[...], 0.0), axis=1, keepdims=True)
    xb = xb_ref[...]
    a = jax.nn.silu(jnp.dot(xb, w1_ref[0], preferred_element_type=F32))
    a = a * jnp.dot(xb, w3_ref[0], preferred_element_type=F32)
    y = jnp.dot(a.astype(BF16), w2_ref[0], preferred_element_type=F32)
    acc_ref[...] += g * y

    @pl.when((e == pl.num_programs(1) - 1) & (f == pl.num_programs(2) - 1))
    def _():
        r = DN_ALPHA * x_ref[...] + acc_ref[...]
        o_ref[...] = _layer_norm(r, lng_ref[...], lnb_ref[...])


def _moe(x, gates, w1, w3, w2, ln_g, ln_b):
    T = x.shape[0]
    tm, tf = 512, FFN_TF
    return pl.pallas_call(
        _moe_kernel,
        grid=(T // tm, N_EXPERTS, D_FF // tf),
        in_specs=[
            pl.BlockSpec((tm, D_MODEL), lambda i, e, f: (i, 0)),
            pl.BlockSpec((tm, LANES), lambda i, e, f: (i, 0)),
            pl.BlockSpec((1, D_MODEL, tf), lambda i, e, f: (e, 0, f)),
            pl.BlockSpec((1, D_MODEL, tf), lambda i, e, f: (e, 0, f)),
            pl.BlockSpec((1, tf, D_MODEL), lambda i, e, f: (e, f, 0)),
            pl.BlockSpec((1, D_MODEL), lambda i, e, f: (0, 0)),
            pl.BlockSpec((1, D_MODEL), lambda i, e, f: (0, 0)),
        ],
        out_specs=pl.BlockSpec((tm, D_MODEL), lambda i, e, f: (i, 0)),
        out_shape=jax.ShapeDtypeStruct((T, D_MODEL), F32),
        scratch_shapes=[pltpu.VMEM((tm, D_MODEL), BF16), pltpu.VMEM((tm, D_MODEL), F32)],
        compiler_params=_cparams("arbitrary", "arbitrary", "arbitrary"),
        name="moe_ln",
    )(x, gates, w1, w3, w2, ln_g, ln_b)


def _qkv_first(a):
    c0, c1 = POOL_WIDTH, POOL_WIDTH + QKV_WIDTH
    return jnp.concatenate([a[..., c0:c1], a[..., :c0], a[..., c1:]], axis=-1)


def kernel(x, ln1_g, ln1_b, w_in, b_in, w_pool, pool_scale, ssm_a_re, ssm_a_im, ssm_log_dt,
           ssm_b_re, ssm_b_im, ssm_c_re, ssm_c_im, ssm_d, w_glu, b_glu, w_branch, w_out,
           ln2_g, ln2_b, ffn_w1, ffn_w3, ffn_w2, moe_router, moe_router_b, moe_w1, moe_w3, moe_w2):
    B, S, D = x.shape
    T = B * S
    assert B == SUBLANES and D == D_MODEL
    h = x.transpose(1, 0, 2).reshape(T, D)

    lr, li, bbr, bbi = _s5_prep(ssm_a_re, ssm_a_im, ssm_log_dt, ssm_b_re, ssm_b_im)
    row = lambda a: a.reshape(1, -1)

    for i in range(DEPTH):
        qkv, z = _inproj(h, _qkv_first(w_in[i]).astype(BF16), row(_qkv_first(b_in[i])), B)
        y_b = _attention(qkv)
        y_a = _pool(z, w_pool[i].astype(BF16), row(pool_scale[i]), B)
        b_mat, c_mat, lam = _s5_layout(lr[i], li[i], bbr[i], bbi[i], ssm_c_re[i], ssm_c_im[i], B)
        y_c = _s5(z, b_mat, lam, c_mat, row(ssm_d[i]), w_glu[i].astype(BF16), row(b_glu[i]), B)
        h = _merge(h, y_a, y_b, y_c, z, w_branch[i].astype(BF16), w_out[i].astype(BF16),
                   row(ln1_g[i]), row(ln1_b[i]))
        j = i // 2
        if i % 2 == 0:
            h = _ffn(h, ffn_w1[j].astype(BF16), ffn_w3[j].astype(BF16), ffn_w2[j].astype(BF16),
                     row(ln2_g[i]), row(ln2_b[i]))
        else:
            rw = jnp.pad(moe_router[j], ((0, 0), (0, LANES - N_EXPERTS)))
            rb = jnp.pad(moe_router_b[j], (0, LANES - N_EXPERTS)).reshape(1, LANES)
            gates = _router(h, rw, rb)
            h = _moe(h, gates, moe_w1[j].astype(BF16), moe_w3[j].astype(BF16),
                     moe_w2[j].astype(BF16), row(ln2_g[i]), row(ln2_b[i]))
    return h.reshape(S, B, D).transpose(1, 0, 2)
```

```python
import functools

import jax
import jax.numpy as jnp
from jax import lax
from jax.experimental import pallas as pl
from jax.experimental.pallas import tpu as pltpu

F32 = jnp.float32
BF16 = jnp.bfloat16

D_MODEL = 1024
DEPTH = 4
POOL_WINDOWS = (2, 4, 8, 16)
POOL_WIDTH = 512
POOL_GROUP = POOL_WIDTH // len(POOL_WINDOWS)
ATT_HEADS = 8
HEAD_DIM = 64
ATT_WIDTH = ATT_HEADS * HEAD_DIM
MOBA_BLOCK = 256
MOBA_TOPK = 3
SSM_WIDTH = 512
SSM_GROUP = 16
SSM_GROUPS = SSM_WIDTH // SSM_GROUP
SSM_STATE = 64
N_BRANCH = 3
BRANCH_WIDTH = 512
IN_WIDTH = POOL_WIDTH + 3 * ATT_WIDTH + SSM_WIDTH + N_BRANCH * D_MODEL
D_FF = 2816
N_EXPERTS = 8
TOP_K = 2
DN_ALPHA = (2 * DEPTH) ** 0.25
LN_EPS = 1e-5

LANES = 128
SUBLANES = 8
VMEM_LIMIT = 48 * 1024 * 1024

NEG = -1e30

COL_POOL, COL_SSM, COL_GATE = 0, 1, 2
QKV_WIDTH = 3 * ATT_WIDTH
Z_WIDTH = IN_WIDTH - QKV_WIDTH
N_QKV_BLOCKS = QKV_WIDTH // 512


def _cparams(*sem):
    return pltpu.CompilerParams(dimension_semantics=sem, vmem_limit_bytes=VMEM_LIMIT)


def _layer_norm(r, g, b):
    mu = jnp.mean(r, axis=-1, keepdims=True)
    c = r - mu
    var = jnp.mean(c * c, axis=-1, keepdims=True)
    return c * lax.rsqrt(var + LN_EPS) * g + b


def _inproj_kernel(x_ref, w_ref, b_ref, qkv_ref, z_ref, xb_ref, acc_scr, *, nbatch):
    j = pl.program_id(1)

    @pl.when(j == 0)
    def _():
        xb_ref[...] = x_ref[...].astype(BF16)

    acc = jnp.dot(xb_ref[...], w_ref[...], preferred_element_type=F32) + b_ref[...]

    @pl.when(j < N_QKV_BLOCKS)
    def _():
        nslab, rows, _ = acc_scr.shape
        steps = rows // nbatch
        for c in range(nslab):
            acc_scr[c] = acc[:, c * LANES:(c + 1) * LANES]
        for b in range(nbatch):
            for c in range(nslab):
                qkv_ref[b, :, c * LANES:(c + 1) * LANES] = (
                    acc_scr[c, pl.ds(b, steps, stride=nbatch), :].astype(qkv_ref.dtype))

    @pl.when(j >= N_QKV_BLOCKS)
    def _():
        z_ref[...] = acc.astype(z_ref.dtype)


def _inproj(x, w, b, nbatch):
    T, K = x.shape
    N = w.shape[1]
    S = T // nbatch
    tm, tn = 1024, 512
    steps = tm // nbatch
    kern = functools.partial(_inproj_kernel, nbatch=nbatch)
    return pl.pallas_call(
        kern,
        grid=(T // tm, N // tn),
        in_specs=[
            pl.BlockSpec((tm, K), lambda i, j: (i, 0)),
            pl.BlockSpec((K, tn), lambda i, j: (0, j)),
            pl.BlockSpec((1, tn), lambda i, j: (0, j)),
        ],
        out_specs=[
            pl.BlockSpec((nbatch, steps, tn), lambda i, j: (0, i, jnp.minimum(j, N_QKV_BLOCKS - 1))),
            pl.BlockSpec((tm, tn), lambda i, j: (i, jnp.maximum(j - N_QKV_BLOCKS, 0))),
        ],
        out_shape=[jax.ShapeDtypeStruct((nbatch, S, QKV_WIDTH), BF16),
                   jax.ShapeDtypeStruct((T, Z_WIDTH), BF16)],
        scratch_shapes=[pltpu.VMEM((tm, K), BF16), pltpu.VMEM((tn // LANES, tm, LANES), F32)],
        compiler_params=_cparams("arbitrary", "arbitrary"),
        name="inproj",
    )(x, w, b)


def _attn_block(i, q_ref, k_ref, v_ref, o_ref, kmh_ref, kml_ref, s_scr, p_scr,
                *, nb, blk, dh, n_heads, topk):
    pair = pl.program_id(1)
    W = 2 * dh
    lane = lax.broadcasted_iota(jnp.int32, (1, W), 1)
    head_of_lane = lane // dh
    nt = (((1,), (1,)), ((), ()))

    if i == 0:
        kmh_ref[...] = jnp.zeros_like(kmh_ref)
        kml_ref[...] = jnp.zeros_like(kml_ref)
        for n in range(nb):
            kb = k_ref[n * blk:(n + 1) * blk, :].astype(F32)
            mean = jnp.sum(kb, axis=0, keepdims=True) * (1.0 / blk)
            for hl in range(2):
                row = jnp.where(head_of_lane == hl, mean, 0.0)
                hi = row.astype(BF16)
                lo = (row - hi.astype(F32)).astype(BF16)
                kmh_ref[hl * nb + n:hl * nb + n + 1, :] = hi
                kml_ref[hl * nb + n:hl * nb + n + 1, :] = lo

    q = q_ref[...] * jnp.asarray(dh ** -0.5, BF16)
    slopes = []
    for hl in range(2):
        h_f = (pair * 2 + hl + 1).astype(F32)
        slopes.append(jnp.exp2(jnp.full((1, blk), -8.0 / n_heads, F32) * h_f))

    if i > 0:
        gate_t = (lax.dot_general(kmh_ref[...], q, nt, preferred_element_type=F32)
                  + lax.dot_general(kml_ref[...], q, nt, preferred_element_type=F32))
        row = lax.broadcasted_iota(jnp.int32, (nb, blk), 0)
        rowdist = ((row - i) * blk).astype(F32)
        bias_parts = []
        for hl in range(2):
            g = gate_t[hl * nb:(hl + 1) * nb, :]
            cnt = jnp.zeros((nb, blk), F32)
            for n2 in range(i):
                r = g[n2:n2 + 1, :]
                cnt = cnt + jnp.where(r > g, 1.0,
                                      jnp.where(r == g, jnp.where(n2 < row, 1.0, 0.0), 0.0))
            sel = jnp.where(row < i, jnp.where(cnt < float(topk), 1.0, 0.0), 0.0)
            bias_parts.append(jnp.where(sel > 0.5, slopes[hl] * rowdist, NEG))
        bias_t = jnp.concatenate(bias_parts + [jnp.zeros((LANES - 2 * nb, blk), F32)], axis=0)
        bias = bias_t.T

    key_pos = lax.broadcasted_iota(jnp.int32, (1, blk), 1).astype(F32)
    r_io = lax.broadcasted_iota(jnp.int32, (blk, blk), 0)
    c_io = lax.broadcasted_iota(jnp.int32, (blk, blk), 1)
    nkeys = (i + 1) * blk

    outs = []
    for hl in range(2):
        qh = jnp.where(head_of_lane == hl, q, jnp.zeros_like(q))
        keybias = slopes[hl] * key_pos
        macc = None
        for n in range(i + 1):
            s = lax.dot_general(qh, k_ref[n * blk:(n + 1) * blk, :], nt,
                                preferred_element_type=F32) + keybias
            if n == i:
                s = jnp.where(c_io <= r_io, s, NEG)
            else:
                c = hl * nb + n
                s = s + bias[:, c:c + 1]
            s_scr[hl, :, n * blk:(n + 1) * blk] = s
            t = jnp.maximum(s[:, :LANES], s[:, LANES:])
            macc = t if macc is None else jnp.maximum(macc, t)
        m = jnp.max(macc, axis=1, keepdims=True)
        lacc = None
        for n in range(i + 1):
            p = jnp.exp(s_scr[hl, :, n * blk:(n + 1) * blk] - m)
            p_scr[hl, :, n * blk:(n + 1) * blk] = p.astype(BF16)
            t = p[:, :LANES] + p[:, LANES:]
            lacc = t if lacc is None else lacc + t
        l = jnp.sum(lacc, axis=1, keepdims=True)
        acc = jnp.dot(p_scr[hl, :, 0:nkeys], v_ref[0:nkeys, :], preferred_element_type=F32)
        outs.append(acc * (1.0 / l))

    o_ref[...] = jnp.where(head_of_lane == 0, outs[0], outs[1]).astype(o_ref.dtype)


def _attn_kernel(*refs, nb, **kw):
    i = pl.program_id(2)
    for k in range(nb):
        pl.when(i == k)(functools.partial(_attn_block, k, *refs, nb=nb, **kw))


def _attention(qkv):
    B, S, _ = qkv.shape
    blk, dh = MOBA_BLOCK, HEAD_DIM
    nb = S // blk
    assert nb * blk == S and nb == SUBLANES and 2 * dh == LANES
    n_pairs = ATT_HEADS // 2
    kern = functools.partial(_attn_kernel, nb=nb, blk=blk, dh=dh, n_heads=ATT_HEADS, topk=MOBA_TOPK)
    return pl.pallas_call(
        kern,
        grid=(B, n_pairs, nb),
        in_specs=[
            pl.BlockSpec((None, blk, LANES), lambda b, p, i: (b, i, p)),
            pl.BlockSpec((None, S, LANES), lambda b, p, i: (b, 0, n_pairs + p)),
            pl.BlockSpec((None, S, LANES), lambda b, p, i: (b, 0, 2 * n_pairs + p)),
        ],
        out_specs=pl.BlockSpec((None, blk, LANES), lambda b, p, i: (b, i, p)),
        out_shape=jax.ShapeDtypeStruct((B, S, ATT_WIDTH), F32),
        scratch_shapes=[
            pltpu.VMEM((LANES, LANES), BF16),
            pltpu.VMEM((LANES, LANES), BF16),
            pltpu.VMEM((2, blk, nb * blk), F32),
            pltpu.VMEM((2, blk, nb * blk), BF16),
        ],
        compiler_params=_cparams("arbitrary", "arbitrary", "arbitrary"),
        name="moba_attn",
    )(qkv, qkv, qkv)


POOL_HALO_STEPS = max(POOL_WINDOWS)


def _pool_kernel(u_ref, w_ref, scale_ref, o_ref, ext_scr, *, steps, nbatch):
    c = pl.program_id(0)
    R = steps * nbatch
    H = POOL_HALO_STEPS * nbatch

    @pl.when(c == 0)
    def _():
        ext_scr[0:H, :] = jnp.zeros((H, POOL_WIDTH), F32)

    @pl.when(c > 0)
    def _():
        ext_scr[0:H, :] = ext_scr[R:R + H, :]

    ext_scr[H:H + R, :] = u_ref[...].astype(F32)

    t_glob = c * steps + lax.broadcasted_iota(jnp.int32, (R, 1), 0) // nbatch
    outs = []
    for gi, w in enumerate(POOL_WINDOWS):
        cols = slice(gi * POOL_GROUP, (gi + 1) * POOL_GROUP)
        u = ext_scr[H:H + R, cols]
        acc = u
        for k in range(1, w):
            acc = acc + ext_scr[H - k * nbatch:H - k * nbatch + R, cols]
        count = jnp.minimum(t_glob + 1, w).astype(F32)
        d = acc / count - u
        outs.append(jnp.dot(d.astype(BF16), w_ref[gi], preferred_element_type=F32))
    y = jnp.concatenate(outs, axis=1) * scale_ref[...]
    o_ref[...] = y.astype(o_ref.dtype)


def _pool(z, w_pool, pool_scale, nbatch):
    T = z.shape[0]
    steps = 64
    R = steps * nbatch
    H = POOL_HALO_STEPS * nbatch
    kern = functools.partial(_pool_kernel, steps=steps, nbatch=nbatch)
    return pl.pallas_call(
        kern,
        grid=(T // R,),
        in_specs=[
            pl.BlockSpec((R, 512), lambda c: (c, COL_POOL)),
            pl.BlockSpec((len(POOL_WINDOWS), POOL_GROUP, POOL_GROUP), lambda c: (0, 0, 0)),
            pl.BlockSpec((1, POOL_WIDTH), lambda c: (0, 0)),
        ],
        out_specs=pl.BlockSpec((R, POOL_WIDTH), lambda c: (c, 0)),
        out_shape=jax.ShapeDtypeStruct((T, POOL_WIDTH), BF16),
        scratch_shapes=[pltpu.VMEM((H + R, POOL_WIDTH), F32)],
        compiler_params=_cparams("arbitrary"),
        name="pool_mixer",
    )(z, w_pool, pool_scale)


def _s5_prep_kernel(ar_ref, ai_ref, ldt_ref, br_ref, bi_ref, lr_ref, li_ref, bbr_ref, bbi_ref):
    ar, ai = ar_ref[...], ai_ref[...]
    dt = jnp.exp(ldt_ref[...])
    mag = jnp.exp(ar * dt)
    lr = mag * jnp.cos(ai * dt)
    li = mag * jnp.sin(ai * dt)
    den = ar * ar + ai * ai
    nr = lr - 1.0
    fr = (nr * ar + li * ai) / den
    fi = (li * ar - nr * ai) / den
    br, bi = br_ref[...], bi_ref[...]
    lr_ref[...] = lr
    li_ref[...] = li
    bbr_ref[...] = fr * br - fi * bi
    bbi_ref[...] = fr * bi + fi * br


def _s5_prep(a_re, a_im, log_dt, b_re, b_im):
    L, G, N = a_re.shape
    P = b_re.shape[-1]
    rows, cols = L * G, N * P
    exp_n = lambda a: jnp.broadcast_to(a[..., None], (L, G, N, P)).reshape(rows, cols)
    ldt = jnp.broadcast_to(log_dt[..., None, None], (L, G, N, P)).reshape(rows, cols)
    shp = jax.ShapeDtypeStruct((rows, cols), F32)
    spec = pl.BlockSpec((rows, cols), lambda: (0, 0))
    lr, li, bbr, bbi = pl.pallas_call(
        _s5_prep_kernel,
        in_specs=[spec] * 5,
        out_specs=[spec] * 4,
        out_shape=[shp] * 4,
        name="s5_discretise",
    )(exp_n(a_re), exp_n(a_im), ldt, b_re.reshape(rows, cols), b_im.reshape(rows, cols))
    lr = lr.reshape(L, G, N, P)[..., 0].reshape(L, G * N)
    li = li.reshape(L, G, N, P)[..., 0].reshape(L, G * N)
    return lr, li, bbr.reshape(L, G, N, P), bbi.reshape(L, G, N, P)


S5_GPB = LANES // SSM_GROUP
S5_NBLK = SSM_WIDTH // LANES
S5_SB = S5_GPB * SSM_STATE
S5_NS = SSM_GROUPS * SSM_STATE


def _s5_layout(lr, li, bbr, bbi, c_re, c_im, nbatch):
    eye = jnp.eye(S5_GPB, dtype=F32)

    def bmat(bb):
        bb = bb.reshape(S5_NBLK, S5_GPB, SSM_STATE, SSM_GROUP)
        m = jnp.einsum('jgnp,gh->jgphn', bb, eye)
        return m.reshape(S5_NBLK, LANES, S5_SB)

    def cmat(c):
        c = c.reshape(S5_NBLK, S5_GPB, SSM_GROUP, SSM_STATE)
        m = jnp.einsum('jgpn,gh->jgnhp', c, eye)
        return m.reshape(S5_NBLK, S5_SB, LANES)

    b_mat = jnp.concatenate([bmat(bbr), bmat(bbi)], axis=2).astype(BF16)
    c_mat = jnp.concatenate([cmat(c_re), -cmat(c_im)], axis=1).astype(BF16)
    lam = jnp.concatenate([jnp.broadcast_to(lr[None, :], (nbatch, S5_NS)),
                           jnp.broadcast_to(li[None, :], (nbatch, S5_NS))], axis=0)
    return b_mat, c_mat, lam


S5_SCAN_COLS = 1024


def _s5_kernel(u_ref, bmat_ref, lam_ref, cmat_ref, d_ref, wglu_ref, bglu_ref, o_ref,
               x_scr, h_scr, *, steps, nbatch):
    @pl.when(pl.program_id(0) == 0)
    def _():
        h_scr[...] = jnp.zeros_like(h_scr)

    u = u_ref[...]
    for j in range(S5_NBLK):
        xj = jnp.dot(u[:, j * LANES:(j + 1) * LANES], bmat_ref[j], preferred_element_type=F32)
        x_scr[:, j * S5_SB:(j + 1) * S5_SB] = xj[:, :S5_SB]
        x_scr[:, S5_NS + j * S5_SB:S5_NS + (j + 1) * S5_SB] = xj[:, S5_SB:]

    for c in range(S5_NS // S5_SCAN_COLS):
        re = slice(c * S5_SCAN_COLS, (c + 1) * S5_SCAN_COLS)
        im = slice(S5_NS + c * S5_SCAN_COLS, S5_NS + (c + 1) * S5_SCAN_COLS)
        lr = lam_ref[0:nbatch, re]
        li = lam_ref[nbatch:2 * nbatch, re]

        def body(t, carry, re=re, im=im, lr=lr, li=li):
            hr, hi = carry
            r0 = pl.multiple_of(t * nbatch, nbatch)
            nhr = lr * hr - li * hi + x_scr[pl.ds(r0, nbatch), re]
            nhi = lr * hi + li * hr + x_scr[pl.ds(r0, nbatch), im]
            x_scr[pl.ds(r0, nbatch), re] = nhr
            x_scr[pl.ds(r0, nbatch), im] = nhi
            return nhr, nhi

        hr, hi = lax.fori_loop(0, steps, body, (h_scr[:, re], h_scr[:, im]), unroll=4)
        h_scr[:, re] = hr
        h_scr[:, im] = hi

    ys = []
    for j in range(S5_NBLK):
        hr = x_scr[:, j * S5_SB:(j + 1) * S5_SB].astype(BF16)
        hi = x_scr[:, S5_NS + j * S5_SB:S5_NS + (j + 1) * S5_SB].astype(BF16)
        ys.append(jnp.dot(hr, cmat_ref[j, 0:S5_SB, :], preferred_element_type=F32)
                  + jnp.dot(hi, cmat_ref[j, S5_SB:2 * S5_SB, :], preferred_element_type=F32))
    y = jnp.concatenate(ys, axis=1) + d_ref[...] * u.astype(F32)
    y = jax.nn.gelu(y, approximate=True)
    z = jnp.dot(y.astype(BF16), wglu_ref[...], preferred_element_type=F32) + bglu_ref[...]
    o_ref[...] = (z[:, :SSM_WIDTH] * jax.nn.sigmoid(z[:, SSM_WIDTH:])).astype(o_ref.dtype)


def _s5(z, b_mat, lam, c_mat, d_skip, w_glu, b_glu, nbatch):
    T = z.shape[0]
    steps = 32
    R = steps * nbatch
    kern = functools.partial(_s5_kernel, steps=steps, nbatch=nbatch)
    const = lambda *shape: pl.BlockSpec(shape, lambda c: (0,) * len(shape))
    return pl.pallas_call(
        kern,
        grid=(T // R,),
        in_specs=[
            pl.BlockSpec((R, 512), lambda c: (c, COL_SSM)),
            const(S5_NBLK, LANES, 2 * S5_SB),
            const(2 * nbatch, S5_NS),
            const(S5_NBLK, 2 * S5_SB, LANES),
            const(1, SSM_WIDTH),
            const(SSM_WIDTH, 2 * SSM_WIDTH),
            const(1, 2 * SSM_WIDTH),
        ],
        out_specs=pl.BlockSpec((R, SSM_WIDTH), lambda c: (c, 0)),
        out_shape=jax.ShapeDtypeStruct((T, SSM_WIDTH), BF16),
        scratch_shapes=[pltpu.VMEM((R, 2 * S5_NS), F32), pltpu.VMEM((nbatch, 2 * S5_NS), F32)],
        compiler_params=_cparams("arbitrary"),
        name="s5_mixer",
    )(z, b_mat, lam, c_mat, d_skip, w_glu, b_glu)


def _merge_kernel(h_ref, ya_ref, yb_ref, yc_ref, g0a, g0b, g1a, g1b, g2a, g2b,
                  wbr_ref, wout_ref, lng_ref, lnb_ref, o_ref, yb_scr):
    nbatch, steps, _ = yb_ref.shape
    nslab = yb_scr.shape[0]
    for b in range(nbatch):
        for c in range(nslab):
            yb_scr[c, pl.ds(b, steps, stride=nbatch), :] = yb_ref[b, :, c * LANES:(c + 1) * LANES]
    yb = jnp.concatenate([yb_scr[c] for c in range(nslab)], axis=1).astype(BF16)
    ys = (ya_ref[...], yb, yc_ref[...])
    gs = ((g0a, g0b), (g1a, g1b), (g2a, g2b))
    half = D_MODEL // 2
    merged = []
    for hcol in range(2):
        acc = None
        for n in range(N_BRANCH):
            proj = jnp.dot(ys[n], wbr_ref[n, :, hcol * half:(hcol + 1) * half],
                           preferred_element_type=F32)
            term = jax.nn.sigmoid(gs[n][hcol][...].astype(F32)) * proj
            acc = term if acc is None else acc + term
        merged.append(acc.astype(BF16))
    merged = jnp.concatenate(merged, axis=1)
    mix = jnp.dot(merged, wout_ref[...], preferred_element_type=F32)
    r = DN_ALPHA * h_ref[...] + mix
    o_ref[...] = _layer_norm(r, lng_ref[...], lnb_ref[...])


def _merge(h, y_a, y_b, y_c, z, w_branch, w_out, ln_g, ln_b):
    T = h.shape[0]
    nbatch = y_b.shape[0]
    tm = 512
    steps = tm // nbatch
    row = lambda w: pl.BlockSpec((tm, w), lambda i: (i, 0))
    gate = lambda k: pl.BlockSpec((tm, 512), lambda i, k=k: (i, COL_GATE + k))
    return pl.pallas_call(
        _merge_kernel,
        grid=(T // tm,),
        in_specs=[row(D_MODEL), row(512),
                  pl.BlockSpec((nbatch, steps, ATT_WIDTH), lambda i: (0, i, 0)),
                  row(512)]
        + [gate(k) for k in range(2 * N_BRANCH)]
        + [
            pl.BlockSpec((N_BRANCH, BRANCH_WIDTH, D_MODEL), lambda i: (0, 0, 0)),
            pl.BlockSpec((D_MODEL, D_MODEL), lambda i: (0, 0)),
            pl.BlockSpec((1, D_MODEL), lambda i: (0, 0)),
            pl.BlockSpec((1, D_MODEL), lambda i: (0, 0)),
        ],
        out_specs=row(D_MODEL),
        out_shape=jax.ShapeDtypeStruct((T, D_MODEL), F32),
        scratch_shapes=[pltpu.VMEM((ATT_WIDTH // LANES, tm, LANES), F32)],
        compiler_params=_cparams("arbitrary"),
        name="merge_ln",
    )(h, y_a, y_b, y_c, z, z, z, z, z, z, w_branch, w_out, ln_g, ln_b)


def _ffn_kernel(x_ref, w1_ref, w3_ref, w2_ref, lng_ref, lnb_ref, o_ref, xb_ref, acc_ref):
    f = pl.program_id(1)

    @pl.when(f == 0)
    def _():
        xb_ref[...] = x_ref[...].astype(BF16)
        acc_ref[...] = jnp.zeros_like(acc_ref)

    xb = xb_ref[...]
    a = jax.nn.silu(jnp.dot(xb, w1_ref[...], preferred_element_type=F32))
    a = a * jnp.dot(xb, w3_ref[...], preferred_element_type=F32)
    acc_ref[...] += jnp.dot(a.astype(BF16), w2_ref[...], preferred_element_type=F32)

    @pl.when(f == pl.num_programs(1) - 1)
    def _():
        r = DN_ALPHA * x_ref[...] + acc_ref[...]
        o_ref[...] = _layer_norm(r, lng_ref[...], lnb_ref[...])


FFN_TF = 1408


def _ffn(x, w1, w3, w2, ln_g, ln_b):
    T = x.shape[0]
    tm, tf = 512, FFN_TF
    return pl.pallas_call(
        _ffn_kernel,
        grid=(T // tm, D_FF // tf),
        in_specs=[
            pl.BlockSpec((tm, D_MODEL), lambda i, f: (i, 0)),
            pl.BlockSpec((D_MODEL, tf), lambda i, f: (0, f)),
            pl.BlockSpec((D_MODEL, tf), lambda i, f: (0, f)),
            pl.BlockSpec((tf, D_MODEL), lambda i, f: (f, 0)),
            pl.BlockSpec((1, D_MODEL), lambda i, f: (0, 0)),
            pl.BlockSpec((1, D_MODEL), lambda i, f: (0, 0)),
        ],
        out_specs=pl.BlockSpec((tm, D_MODEL), lambda i, f: (i, 0)),
        out_shape=jax.ShapeDtypeStruct((T, D_MODEL), F32),
        scratch_shapes=[pltpu.VMEM((tm, D_MODEL), BF16), pltpu.VMEM((tm, D_MODEL), F32)],
        compiler_params=_cparams("arbitrary", "arbitrary"),
        name="ffn_ln",
    )(x, w1, w3, w2, ln_g, ln_b)


def _router_kernel(x_ref, w_ref, b_ref, idx_ref, wgt_ref):
    logits = jnp.dot(x_ref[...], w_ref[...], preferred_element_type=F32,
                     precision=lax.Precision.HIGHEST) + b_ref[...]
    lane = lax.broadcasted_iota(jnp.int32, logits.shape, 1)
    l1 = jnp.where(lane < N_EXPERTS, logits, -jnp.inf)
    m1 = jnp.max(l1, axis=1, keepdims=True)
    i1 = jnp.min(jnp.where(l1 == m1, lane, LANES), axis=1, keepdims=True)
    l2 = jnp.where(lane == i1, -jnp.inf, l1)
    m2 = jnp.max(l2, axis=1, keepdims=True)
    i2 = jnp.min(jnp.where(l2 == m2, lane, LANES), axis=1, keepdims=True)
    e = jnp.exp(m2 - m1)
    w_top = 1.0 / (1.0 + e)
    w_sec = e / (1.0 + e)
    idx_ref[...] = jnp.where(lane == 0, i1, jnp.where(lane == 1, i2, 0))
    wgt_ref[...] = jnp.where(lane == 0, w_top, jnp.where(lane == 1, w_sec, 0.0))


def _router(x, w, b):
    T = x.shape[0]
    tm = 1024
    out = pl.BlockSpec((tm, LANES), lambda i: (i, 0))
    return pl.pallas_call(
        _router_kernel,
        grid=(T // tm,),
        in_specs=[
            pl.BlockSpec((tm, D_MODEL), lambda i: (i, 0)),
            pl.BlockSpec((D_MODEL, LANES), lambda i: (0, 0)),
            pl.BlockSpec((1, LANES), lambda i: (0, 0)),
        ],
        out_specs=[out, out],
        out_shape=[jax.ShapeDtypeStruct((T, LANES), jnp.int32),
                   jax.ShapeDtypeStruct((T, LANES), F32)],
        compiler_params=_cparams("arbitrary"),
        name="moe_router",
    )(x, w, b)


MOE_TM = 512


def _route_plan(idx, wgt, T):
    n_assign = TOP_K * T
    n_tiles = n_assign // MOE_TM + N_EXPERTS
    e_flat = idx.T.reshape(-1)
    w_flat = wgt.T.reshape(-1)
    order = jnp.argsort(e_flat, stable=True).astype(jnp.int32)
    counts = jnp.sum((e_flat[:, None] == jnp.arange(N_EXPERTS)[None, :]).astype(jnp.int32), axis=0)
    group_start = jnp.cumsum(counts) - counts
    tiles_per = (counts + MOE_TM - 1) // MOE_TM
    tile_end = jnp.cumsum(tiles_per)
    tile_start = tile_end - tiles_per
    tile_id = jnp.arange(n_tiles, dtype=jnp.int32)
    used = tile_id < tile_end[-1]
    tile_expert = jnp.minimum(jnp.sum((tile_id[:, None] >= tile_end[None, :]).astype(jnp.int32), axis=1),
                              N_EXPERTS - 1)
    tile_first = (tile_id - tile_start[tile_expert]) * MOE_TM
    tile_rows = jnp.where(used, jnp.clip(counts[tile_expert] - tile_first, 0, MOE_TM), 0)
    last_expert = tile_expert[jnp.maximum(tile_end[-1] - 1, 0)]
    tile_expert = jnp.where(used, tile_expert, last_expert).astype(jnp.int32)
    j = jnp.arange(MOE_TM, dtype=jnp.int32)[None, :]
    valid = j < tile_rows[:, None]
    pos = jnp.clip(group_start[tile_expert][:, None] + tile_first[:, None] + j, 0, n_assign - 1)
    a = jnp.where(valid, order[pos], 0)
    src_tok = jnp.where(a >= T, a - T, a).astype(jnp.int32)
    row_w = jnp.where(valid, w_flat[a], 0.0)
    return (tile_expert, tile_rows.astype(jnp.int32), src_tok[:, None, :], a.astype(jnp.int32)[:, None, :],
            row_w[:, :, None])


def _moe_kernel(te_ref, tn_ref, src_ref, dst_ref, roww_ref, x_hbm, w1_ref, w3_ref, w2_ref, buf_hbm,
                xg, xb, acc, y_scr, gsem, ssem):
    t = pl.program_id(0)
    f = pl.program_id(1)
    nrows = tn_ref[t]

    def gather_copy(r):
        return pltpu.make_async_copy(x_hbm.at[pl.ds(src_ref[0, r], 1), :], xg.at[pl.ds(r, 1), :], gsem)

    def scatter_copy(r):
        return pltpu.make_async_copy(y_scr.at[pl.ds(r, 1), :], buf_hbm.at[pl.ds(dst_ref[0, r], 1), :], ssem)

    def for_rows(fn):
        def body(r, c):
            fn(r)
            return c
        lax.fori_loop(0, nrows, body, 0)

    @pl.when((t == 0) & (f == 0))
    def _():
        xg[...] = jnp.zeros_like(xg)

    @pl.when((f == 0) & (nrows > 0))
    def _():
        for_rows(lambda r: gather_copy(r).start())
        for_rows(lambda r: gather_copy(r).wait())
        xb[...] = xg[...].astype(BF16)

    @pl.when(nrows > 0)
    def _():
        x = xb[...]
        a = jax.nn.silu(jnp.dot(x, w1_ref[...], preferred_element_type=F32))
        a = a * jnp.dot(x, w3_ref[...], preferred_element_type=F32)
        y = jnp.dot(a.astype(BF16), w2_ref[...], preferred_element_type=F32)

        @pl.when(f == 0)
        def _():
            acc[...] = y

        @pl.when(f > 0)
        def _():
            acc[...] += y

    @pl.when((f == pl.num_programs(1) - 1) & (nrows > 0))
    def _():
        y_scr[...] = acc[...] * roww_ref[...]
        for_rows(lambda r: scatter_copy(r).start())
        for_rows(lambda r: scatter_copy(r).wait())


def _moe_experts(x, plan, w1, w3, w2):
    T = x.shape[0]
    tile_expert, tile_rows, src_tok, dst_row, row_w = plan
    n_tiles = tile_expert.shape[0]
    tf = FFN_TF
    smem_rows = pl.BlockSpec((None, 1, MOE_TM), lambda t, f, te, tn: (t, 0, 0), memory_space=pltpu.SMEM)
    grid_spec = pltpu.PrefetchScalarGridSpec(
        num_scalar_prefetch=2,
        grid=(n_tiles, D_FF // tf),
        in_specs=[
            smem_rows,
            smem_rows,
            pl.BlockSpec((None, MOE_TM, 1), lambda t, f, te, tn: (t, 0, 0)),
            pl.BlockSpec(memory_space=pl.ANY),
            pl.BlockSpec((None, D_MODEL, tf), lambda t, f, te, tn: (te[t], 0, f)),
            pl.BlockSpec((None, D_MODEL, tf), lambda t, f, te, tn: (te[t], 0, f)),
            pl.BlockSpec((None, tf, D_MODEL), lambda t, f, te, tn: (te[t], f, 0)),
        ],
        out_specs=pl.BlockSpec(memory_space=pl.ANY),
        scratch_shapes=[
            pltpu.VMEM((MOE_TM, D_MODEL), F32),
            pltpu.VMEM((MOE_TM, D_MODEL), BF16),
            pltpu.VMEM((MOE_TM, D_MODEL), F32),
            pltpu.VMEM((MOE_TM, D_MODEL), F32),
            pltpu.SemaphoreType.DMA(()),
            pltpu.SemaphoreType.DMA(()),
        ],
    )
    return pl.pallas_call(
        _moe_kernel,
        grid_spec=grid_spec,
        out_shape=jax.ShapeDtypeStruct((TOP_K * T, D_MODEL), F32),
        compiler_params=_cparams("arbitrary", "arbitrary"),
        name="moe_experts",
    )(tile_expert, tile_rows, src_tok, dst_row, row_w, x, w1, w3, w2)


def _moe_combine_kernel(x_ref, y0_ref, y1_ref, lng_ref, lnb_ref, o_ref):
    r = DN_ALPHA * x_ref[...] + (y0_ref[...] + y1_ref[...])
    o_ref[...] = _layer_norm(r, lng_ref[...], lnb_ref[...])


def _moe_combine(x, buf, ln_g, ln_b):
    T = x.shape[0]
    tm = 1024
    nblk = T // tm
    return pl.pallas_call(
        _moe_combine_kernel,
        grid=(nblk,),
        in_specs=[
            pl.BlockSpec((tm, D_MODEL), lambda i: (i, 0)),
            pl.BlockSpec((tm, D_MODEL), lambda i: (i, 0)),
            pl.BlockSpec((tm, D_MODEL), lambda i: (i + nblk, 0)),
            pl.BlockSpec((1, D_MODEL), lambda i: (0, 0)),
            pl.BlockSpec((1, D_MODEL), lambda i: (0, 0)),
        ],
        out_specs=pl.BlockSpec((tm, D_MODEL), lambda i: (i, 0)),
        out_shape=jax.ShapeDtypeStruct((T, D_MODEL), F32),
        compiler_params=_cparams("arbitrary"),
        name="moe_combine_ln",
    )(x, buf, buf, ln_g, ln_b)


def _qkv_first(a):
    c0, c1 = POOL_WIDTH, POOL_WIDTH + QKV_WIDTH
    return jnp.concatenate([a[..., c0:c1], a[..., :c0], a[..., c1:]], axis=-1)


def kernel(x, ln1_g, ln1_b, w_in, b_in, w_pool, pool_scale, ssm_a_re, ssm_a_im, ssm_log_dt,
           ssm_b_re, ssm_b_im, ssm_c_re, ssm_c_im, ssm_d, w_glu, b_glu, w_branch, w_out,
           ln2_g, ln2_b, ffn_w1, ffn_w3, ffn_w2, moe_router, moe_router_b, moe_w1, moe_w3, moe_w2):
    B, S, D = x.shape
    T = B * S
    assert B == SUBLANES and D == D_MODEL
    h = x.transpose(1, 0, 2).reshape(T, D)

    lr, li, bbr, bbi = _s5_prep(ssm_a_re, ssm_a_im, ssm_log_dt, ssm_b_re, ssm_b_im)
    row = lambda a: a.reshape(1, -1)

    for i in range(DEPTH):
        qkv, z = _inproj(h, _qkv_first(w_in[i]).astype(BF16), row(_qkv_first(b_in[i])), B)
        y_b = _attention(qkv)
        y_a = _pool(z, w_pool[i].astype(BF16), row(pool_scale[i]), B)
        b_mat, c_mat, lam = _s5_layout(lr[i], li[i], bbr[i], bbi[i], ssm_c_re[i], ssm_c_im[i], B)
        y_c = _s5(z, b_mat, lam, c_mat, row(ssm_d[i]), w_glu[i].astype(BF16), row(b_glu[i]), B)
        h = _merge(h, y_a, y_b, y_c, z, w_branch[i].astype(BF16), w_out[i].astype(BF16),
                   row(ln1_g[i]), row(ln1_b[i]))
        j = i // 2
        if i % 2 == 0:
            h = _ffn(h, ffn_w1[j].astype(BF16), ffn_w3[j].astype(BF16), ffn_w2[j].astype(BF16),
                     row(ln2_g[i]), row(ln2_b[i]))
        else:
            rw = jnp.pad(moe_router[j], ((0, 0), (0, LANES - N_EXPERTS)))
            rb = jnp.pad(moe_router_b[j], (0, LANES - N_EXPERTS)).reshape(1, LANES)
            idx, wgt = _router(h, rw, rb)
            plan = _route_plan(idx[:, :TOP_K], wgt[:, :TOP_K], T)
            buf = _moe_experts(h, plan, moe_w1[j].astype(BF16), moe_w3[j].astype(BF16),
                               moe_w2[j].astype(BF16))
            h = _moe_combine(h, buf, row(ln2_g[i]), row(ln2_b[i]))
    return h.reshape(S, B, D).transpose(1, 0, 2)
```

```python
import functools

import jax
import jax.numpy as jnp
from jax import lax
from jax.experimental import pallas as pl
from jax.experimental.pallas import tpu as pltpu

F32 = jnp.float32
BF16 = jnp.bfloat16

D_MODEL = 1024
DEPTH = 4
POOL_WINDOWS = (2, 4, 8, 16)
POOL_WIDTH = 512
POOL_GROUP = POOL_WIDTH // len(POOL_WINDOWS)
ATT_HEADS = 8
HEAD_DIM = 64
ATT_WIDTH = ATT_HEADS * HEAD_DIM
MOBA_BLOCK = 256
MOBA_TOPK = 3
SSM_WIDTH = 512
SSM_GROUP = 16
SSM_GROUPS = SSM_WIDTH // SSM_GROUP
SSM_STATE = 64
N_BRANCH = 3
BRANCH_WIDTH = 512
IN_WIDTH = POOL_WIDTH + 3 * ATT_WIDTH + SSM_WIDTH + N_BRANCH * D_MODEL
D_FF = 2816
N_EXPERTS = 8
TOP_K = 2
DN_ALPHA = (2 * DEPTH) ** 0.25
LN_EPS = 1e-5

LANES = 128
SUBLANES = 8
VMEM_LIMIT = 48 * 1024 * 1024

NEG = -1e30

COL_POOL, COL_SSM, COL_GATE = 0, 1, 2
QKV_WIDTH = 3 * ATT_WIDTH
Z_WIDTH = IN_WIDTH - QKV_WIDTH
N_QKV_BLOCKS = QKV_WIDTH // 512


def _cparams(*sem):
    return pltpu.CompilerParams(dimension_semantics=sem, vmem_limit_bytes=VMEM_LIMIT)


def _layer_norm(r, g, b):
    mu = jnp.mean(r, axis=-1, keepdims=True)
    c = r - mu
    var = jnp.mean(c * c, axis=-1, keepdims=True)
    return c * lax.rsqrt(var + LN_EPS) * g + b


def _inproj_kernel(x_ref, w_ref, b_ref, qkv_ref, z_ref, xb_ref, acc_scr, *, nbatch):
    j = pl.program_id(1)

    @pl.when(j == 0)
    def _():
        xb_ref[...] = x_ref[...].astype(BF16)

    acc = jnp.dot(xb_ref[...], w_ref[...], preferred_element_type=F32) + b_ref[...]

    @pl.when(j < N_QKV_BLOCKS)
    def _():
        nslab, rows, _ = acc_scr.shape
        steps = rows // nbatch
        for c in range(nslab):
            acc_scr[c] = acc[:, c * LANES:(c + 1) * LANES]
        for b in range(nbatch):
            for c in range(nslab):
                qkv_ref[b, :, c * LANES:(c + 1) * LANES] = (
                    acc_scr[c, pl.ds(b, steps, stride=nbatch), :].astype(qkv_ref.dtype))

    @pl.when(j >= N_QKV_BLOCKS)
    def _():
        z_ref[...] = acc.astype(z_ref.dtype)


def _qkv_first(j):
    return jnp.where(j < N_QKV_BLOCKS, j + 1, jnp.where(j == N_QKV_BLOCKS, 0, j))


def _inproj(x, w, b, nbatch):
    T, K = x.shape
    N = w.shape[1]
    S = T // nbatch
    tm, tn = 1024, 512
    steps = tm // nbatch
    kern = functools.partial(_inproj_kernel, nbatch=nbatch)
    return pl.pallas_call(
        kern,
        grid=(T // tm, N // tn),
        in_specs=[
            pl.BlockSpec((tm, K), lambda i, j: (i, 0)),
            pl.BlockSpec((K, tn), lambda i, j: (0, _qkv_first(j))),
            pl.BlockSpec((1, tn), lambda i, j: (0, _qkv_first(j))),
        ],
        out_specs=[
            pl.BlockSpec((nbatch, steps, tn), lambda i, j: (0, i, jnp.minimum(j, N_QKV_BLOCKS - 1))),
            pl.BlockSpec((tm, tn), lambda i, j: (i, jnp.maximum(j - N_QKV_BLOCKS, 0))),
        ],
        out_shape=[jax.ShapeDtypeStruct((nbatch, S, QKV_WIDTH), BF16),
                   jax.ShapeDtypeStruct((T, Z_WIDTH), BF16)],
        scratch_shapes=[pltpu.VMEM((tm, K), BF16), pltpu.VMEM((tn // LANES, tm, LANES), F32)],
        compiler_params=_cparams("arbitrary", "arbitrary"),
        name="inproj",
    )(x, w, b)


def _attn_block(i, q_ref, k_ref, v_ref, o_ref, kmh_ref, kml_ref, s_scr, p_scr,
                *, nb, blk, dh, n_heads, topk):
    pair = pl.program_id(1)
    W = 2 * dh
    lane = lax.broadcasted_iota(jnp.int32, (1, W), 1)
    head_of_lane = lane // dh
    nt = (((1,), (1,)), ((), ()))

    if i == 0:
        kmh_ref[...] = jnp.zeros_like(kmh_ref)
        kml_ref[...] = jnp.zeros_like(kml_ref)
        for n in range(nb):
            kb = k_ref[n * blk:(n + 1) * blk, :].astype(F32)
            mean = jnp.sum(kb, axis=0, keepdims=True) * (1.0 / blk)
            for hl in range(2):
                row = jnp.where(head_of_lane == hl, mean, 0.0)
                hi = row.astype(BF16)
                lo = (row - hi.astype(F32)).astype(BF16)
                kmh_ref[hl * nb + n:hl * nb + n + 1, :] = hi
                kml_ref[hl * nb + n:hl * nb + n + 1, :] = lo

    par = i % 2
    q = q_ref[i * blk:(i + 1) * blk, :] * jnp.asarray(dh ** -0.5, BF16)
    slopes = []
    for hl in range(2):
        h_f = (pair * 2 + hl + 1).astype(F32)
        slopes.append(jnp.exp2(jnp.full((1, blk), -8.0 / n_heads, F32) * h_f))

    if i > 0:
        gate_t = (lax.dot_general(kmh_ref[...], q, nt, preferred_element_type=F32)
                  + lax.dot_general(kml_ref[...], q, nt, preferred_element_type=F32))
        row = lax.broadcasted_iota(jnp.int32, (nb, blk), 0)
        rowdist = ((row - i) * blk).astype(F32)
        bias_parts = []
        for hl in range(2):
            g = gate_t[hl * nb:(hl + 1) * nb, :]
            cnt = jnp.zeros((nb, blk), F32)
            for n2 in range(i):
                r = g[n2:n2 + 1, :]
                cnt = cnt + jnp.where(r > g, 1.0,
                                      jnp.where(r == g, jnp.where(n2 < row, 1.0, 0.0), 0.0))
            sel = jnp.where(row < i, jnp.where(cnt < float(topk), 1.0, 0.0), 0.0)
            bias_parts.append(jnp.where(sel > 0.5, slopes[hl] * rowdist, NEG))
        bias_t = jnp.concatenate(bias_parts + [jnp.zeros((LANES - 2 * nb, blk), F32)], axis=0)
        bias = bias_t.T

    key_pos = lax.broadcasted_iota(jnp.int32, (1, blk), 1).astype(F32)
    r_io = lax.broadcasted_iota(jnp.int32, (blk, blk), 0)
    c_io = lax.broadcasted_iota(jnp.int32, (blk, blk), 1)
    nkeys = (i + 1) * blk

    outs = []
    for hl in range(2):
        qh = jnp.where(head_of_lane == hl, q, jnp.zeros_like(q))
        keybias = slopes[hl] * key_pos
        macc = None
        for n in range(i + 1):
            s = lax.dot_general(qh, k_ref[n * blk:(n + 1) * blk, :], nt,
                                preferred_element_type=F32) + keybias
            if n == i:
                s = jnp.where(c_io <= r_io, s, NEG)
            else:
                c = hl * nb + n
                s = s + bias[:, c:c + 1]
            s_scr[par, hl, :, n * blk:(n + 1) * blk] = s
            t = jnp.maximum(s[:, :LANES], s[:, LANES:])
            macc = t if macc is None else jnp.maximum(macc, t)
        m = jnp.max(macc, axis=1, keepdims=True)
        lacc = None
        for n in range(i + 1):
            p = jnp.exp(s_scr[par, hl, :, n * blk:(n + 1) * blk] - m)
            p_scr[par, hl, :, n * blk:(n + 1) * blk] = p.astype(BF16)
            t = p[:, :LANES] + p[:, LANES:]
            lacc = t if lacc is None else lacc + t
        l = jnp.sum(lacc, axis=1, keepdims=True)
        acc = jnp.dot(p_scr[par, hl, :, 0:nkeys], v_ref[0:nkeys, :], preferred_element_type=F32)
        outs.append(acc * (1.0 / l))

    o_ref[i * blk:(i + 1) * blk, :] = (
        jnp.where(head_of_lane == 0, outs[0], outs[1]).astype(o_ref.dtype))


def _attn_kernel(*refs, nb, **kw):
    for i in range(nb):
        _attn_block(i, *refs, nb=nb, **kw)


def _attention(qkv):
    B, S, _ = qkv.shape
    blk, dh = MOBA_BLOCK, HEAD_DIM
    nb = S // blk
    assert nb * blk == S and nb == SUBLANES and 2 * dh == LANES
    n_pairs = ATT_HEADS // 2
    kern = functools.partial(_attn_kernel, nb=nb, blk=blk, dh=dh, n_heads=ATT_HEADS, topk=MOBA_TOPK)
    return pl.pallas_call(
        kern,
        grid=(B, n_pairs),
        in_specs=[
            pl.BlockSpec((None, S, LANES), lambda b, p: (b, 0, p)),
            pl.BlockSpec((None, S, LANES), lambda b, p: (b, 0, n_pairs + p)),
            pl.BlockSpec((None, S, LANES), lambda b, p: (b, 0, 2 * n_pairs + p)),
        ],
        out_specs=pl.BlockSpec((None, S, LANES), lambda b, p: (b, 0, p)),
        out_shape=jax.ShapeDtypeStruct((B, S, ATT_WIDTH), F32),
        scratch_shapes=[
            pltpu.VMEM((LANES, LANES), BF16),
            pltpu.VMEM((LANES, LANES), BF16),
            pltpu.VMEM((2, 2, blk, nb * blk), F32),
            pltpu.VMEM((2, 2, blk, nb * blk), BF16),
        ],
        compiler_params=_cparams("arbitrary", "arbitrary"),
        name="moba_attn",
    )(qkv, qkv, qkv)


POOL_HALO_STEPS = max(POOL_WINDOWS)


def _pool_kernel(u_ref, w_ref, scale_ref, o_ref, ext_scr, *, steps, nbatch):
    c = pl.program_id(0)
    R = steps * nbatch
    H = POOL_HALO_STEPS * nbatch

    @pl.when(c == 0)
    def _():
        ext_scr[0:H, :] = jnp.zeros((H, POOL_WIDTH), F32)

    @pl.when(c > 0)
    def _():
        ext_scr[0:H, :] = ext_scr[R:R + H, :]

    ext_scr[H:H + R, :] = u_ref[...].astype(F32)

    t_glob = c * steps + lax.broadcasted_iota(jnp.int32, (R, 1), 0) // nbatch
    outs = []
    for gi, w in enumerate(POOL_WINDOWS):
        cols = slice(gi * POOL_GROUP, (gi + 1) * POOL_GROUP)
        u = ext_scr[H:H + R, cols]
        acc = u
        for k in range(1, w):
            acc = acc + ext_scr[H - k * nbatch:H - k * nbatch + R, cols]
        count = jnp.minimum(t_glob + 1, w).astype(F32)
        d = acc / count - u
        outs.append(jnp.dot(d.astype(BF16), w_ref[gi], preferred_element_type=F32))
    y = jnp.concatenate(outs, axis=1) * scale_ref[...]
    o_ref[...] = y.astype(o_ref.dtype)


def _pool(z, w_pool, pool_scale, nbatch):
    T = z.shape[0]
    steps = 64
    R = steps * nbatch
    H = POOL_HALO_STEPS * nbatch
    kern = functools.partial(_pool_kernel, steps=steps, nbatch=nbatch)
    return pl.pallas_call(
        kern,
        grid=(T // R,),
        in_specs=[
            pl.BlockSpec((R, 512), lambda c: (c, COL_POOL)),
            pl.BlockSpec((len(POOL_WINDOWS), POOL_GROUP, POOL_GROUP), lambda c: (0, 0, 0)),
            pl.BlockSpec((1, POOL_WIDTH), lambda c: (0, 0)),
        ],
        out_specs=pl.BlockSpec((R, POOL_WIDTH), lambda c: (c, 0)),
        out_shape=jax.ShapeDtypeStruct((T, POOL_WIDTH), BF16),
        scratch_shapes=[pltpu.VMEM((H + R, POOL_WIDTH), F32)],
        compiler_params=_cparams("arbitrary"),
        name="pool_mixer",
    )(z, w_pool, pool_scale)


def _s5_prep_kernel(ar_ref, ai_ref, ldt_ref, br_ref, bi_ref, lr_ref, li_ref, bbr_ref, bbi_ref):
    ar, ai = ar_ref[...], ai_ref[...]
    dt = jnp.exp(ldt_ref[...])
    mag = jnp.exp(ar * dt)
    lr = mag * jnp.cos(ai * dt)
    li = mag * jnp.sin(ai * dt)
    den = ar * ar + ai * ai
    nr = lr - 1.0
    fr = (nr * ar + li * ai) / den
    fi = (li * ar - nr * ai) / den
    br, bi = br_ref[...], bi_ref[...]
    lr_ref[...] = lr
    li_ref[...] = li
    bbr_ref[...] = fr * br - fi * bi
    bbi_ref[...] = fr * bi + fi * br


def _s5_prep(a_re, a_im, log_dt, b_re, b_im):
    L, G, N = a_re.shape
    P = b_re.shape[-1]
    rows, cols = L * G, N * P
    exp_n = lambda a: jnp.broadcast_to(a[..., None], (L, G, N, P)).reshape(rows, cols)
    ldt = jnp.broadcast_to(log_dt[..., None, None], (L, G, N, P)).reshape(rows, cols)
    shp = jax.ShapeDtypeStruct((rows, cols), F32)
    spec = pl.BlockSpec((rows, cols), lambda: (0, 0))
    lr, li, bbr, bbi = pl.pallas_call(
        _s5_prep_kernel,
        in_specs=[spec] * 5,
        out_specs=[spec] * 4,
        out_shape=[shp] * 4,
        name="s5_discretise",
    )(exp_n(a_re), exp_n(a_im), ldt, b_re.reshape(rows, cols), b_im.reshape(rows, cols))
    lr = lr.reshape(L, G, N, P)[..., 0].reshape(L, G * N)
    li = li.reshape(L, G, N, P)[..., 0].reshape(L, G * N)
    return lr, li, bbr.reshape(L, G, N, P), bbi.reshape(L, G, N, P)


S5_GPB = LANES // SSM_GROUP
S5_NBLK = SSM_WIDTH // LANES
S5_SB = S5_GPB * SSM_STATE
S5_NS = SSM_GROUPS * SSM_STATE


def _s5_layout(lr, li, bbr, bbi, c_re, c_im, nbatch):
    eye = jnp.eye(S5_GPB, dtype=F32)

    def bmat(bb):
        bb = bb.reshape(S5_NBLK, S5_GPB, SSM_STATE, SSM_GROUP)
        m = jnp.einsum('jgnp,gh->jgphn', bb, eye)
        return m.reshape(S5_NBLK, LANES, S5_SB)

    def cmat(c):
        c = c.reshape(S5_NBLK, S5_GPB, SSM_GROUP, SSM_STATE)
        m = jnp.einsum('jgpn,gh->jgnhp', c, eye)
        return m.reshape(S5_NBLK, S5_SB, LANES)

    b_mat = jnp.concatenate([bmat(bbr), bmat(bbi)], axis=2).astype(BF16)
    c_mat = jnp.concatenate([cmat(c_re), -cmat(c_im)], axis=1).astype(BF16)
    lam = jnp.concatenate([jnp.broadcast_to(lr[None, :], (nbatch, S5_NS)),
                           jnp.broadcast_to(li[None, :], (nbatch, S5_NS))], axis=0)
    return b_mat, c_mat, lam


S5_SCAN_COLS = 1024


def _s5_kernel(u_ref, bmat_ref, lam_ref, cmat_ref, d_ref, wglu_ref, bglu_ref, o_ref,
               x_scr, h_scr, *, steps, nbatch):
    @pl.when(pl.program_id(0) == 0)
    def _():
        h_scr[...] = jnp.zeros_like(h_scr)

    u = u_ref[...]
    for j in range(S5_NBLK):
        xj = jnp.dot(u[:, j * LANES:(j + 1) * LANES], bmat_ref[j], preferred_element_type=F32)
        x_scr[:, j * S5_SB:(j + 1) * S5_SB] = xj[:, :S5_SB]
        x_scr[:, S5_NS + j * S5_SB:S5_NS + (j + 1) * S5_SB] = xj[:, S5_SB:]

    for c in range(S5_NS // S5_SCAN_COLS):
        re = slice(c * S5_SCAN_COLS, (c + 1) * S5_SCAN_COLS)
        im = slice(S5_NS + c * S5_SCAN_COLS, S5_NS + (c + 1) * S5_SCAN_COLS)
        lr = lam_ref[0:nbatch, re]
        li = lam_ref[nbatch:2 * nbatch, re]

        def body(t, carry, re=re, im=im, lr=lr, li=li):
            hr, hi = carry
            r0 = pl.multiple_of(t * nbatch, nbatch)
            nhr = lr * hr - li * hi + x_scr[pl.ds(r0, nbatch), re]
            nhi = lr * hi + li * hr + x_scr[pl.ds(r0, nbatch), im]
            x_scr[pl.ds(r0, nbatch), re] = nhr
            x_scr[pl.ds(r0, nbatch), im] = nhi
            return nhr, nhi

        hr, hi = lax.fori_loop(0, steps, body, (h_scr[:, re], h_scr[:, im]), unroll=4)
        h_scr[:, re] = hr
        h_scr[:, im] = hi

    ys = []
    for j in range(S5_NBLK):
        hr = x_scr[:, j * S5_SB:(j + 1) * S5_SB].astype(BF16)
        hi = x_scr[:, S5_NS + j * S5_SB:S5_NS + (j + 1) * S5_SB].astype(BF16)
        ys.append(jnp.dot(hr, cmat_ref[j, 0:S5_SB, :], preferred_element_type=F32)
                  + jnp.dot(hi, cmat_ref[j, S5_SB:2 * S5_SB, :], preferred_element_type=F32))
    y = jnp.concatenate(ys, axis=1) + d_ref[...] * u.astype(F32)
    y = jax.nn.gelu(y, approximate=True)
    z = jnp.dot(y.astype(BF16), wglu_ref[...], preferred_element_type=F32) + bglu_ref[...]
    o_ref[...] = (z[:, :SSM_WIDTH] * jax.nn.sigmoid(z[:, SSM_WIDTH:])).astype(o_ref.dtype)


def _s5(z, b_mat, lam, c_mat, d_skip, w_glu, b_glu, nbatch):
    T = z.shape[0]
    steps = 32
    R = steps * nbatch
    kern = functools.partial(_s5_kernel, steps=steps, nbatch=nbatch)
    const = lambda *shape: pl.BlockSpec(shape, lambda c: (0,) * len(shape))
    return pl.pallas_call(
        kern,
        grid=(T // R,),
        in_specs=[
            pl.BlockSpec((R, 512), lambda c: (c, COL_SSM)),
            const(S5_NBLK, LANES, 2 * S5_SB),
            const(2 * nbatch, S5_NS),
            const(S5_NBLK, 2 * S5_SB, LANES),
            const(1, SSM_WIDTH),
            const(SSM_WIDTH, 2 * SSM_WIDTH),
            const(1, 2 * SSM_WIDTH),
        ],
        out_specs=pl.BlockSpec((R, SSM_WIDTH), lambda c: (c, 0)),
        out_shape=jax.ShapeDtypeStruct((T, SSM_WIDTH), BF16),
        scratch_shapes=[pltpu.VMEM((R, 2 * S5_NS), F32), pltpu.VMEM((nbatch, 2 * S5_NS), F32)],
        compiler_params=_cparams("arbitrary"),
        name="s5_mixer",
    )(z, b_mat, lam, c_mat, d_skip, w_glu, b_glu)


def _merge_kernel(h_ref, ya_ref, yb_ref, yc_ref, g0a, g0b, g1a, g1b, g2a, g2b,
                  wbr_ref, wout_ref, lng_ref, lnb_ref, o_ref, yb_scr):
    nbatch, steps, _ = yb_ref.shape
    nslab = yb_scr.shape[0]
    for b in range(nbatch):
        for c in range(nslab):
            yb_scr[c, pl.ds(b, steps, stride=nbatch), :] = yb_ref[b, :, c * LANES:(c + 1) * LANES]
    yb = jnp.concatenate([yb_scr[c] for c in range(nslab)], axis=1).astype(BF16)
    ys = (ya_ref[...], yb, yc_ref[...])
    gs = ((g0a, g0b), (g1a, g1b), (g2a, g2b))
    half = D_MODEL // 2
    merged = []
    for hcol in range(2):
        acc = None
        for n in range(N_BRANCH):
            proj = jnp.dot(ys[n], wbr_ref[n, :, hcol * half:(hcol + 1) * half],
                           preferred_element_type=F32)
            term = jax.nn.sigmoid(gs[n][hcol][...].astype(F32)) * proj
            acc = term if acc is None else acc + term
        merged.append(acc.astype(BF16))
    merged = jnp.concatenate(merged, axis=1)
    mix = jnp.dot(merged, wout_ref[...], preferred_element_type=F32)
    r = DN_ALPHA * h_ref[...] + mix
    o_ref[...] = _layer_norm(r, lng_ref[...], lnb_ref[...])


def _merge(h, y_a, y_b, y_c, z, w_branch, w_out, ln_g, ln_b):
    T = h.shape[0]
    nbatch = y_b.shape[0]
    tm = 512
    steps = tm // nbatch
    row = lambda w: pl.BlockSpec((tm, w), lambda i: (i, 0))
    gate = lambda k: pl.BlockSpec((tm, 512), lambda i, k=k: (i, COL_GATE + k))
    return pl.pallas_call(
        _merge_kernel,
        grid=(T // tm,),
        in_specs=[row(D_MODEL), row(512),
                  pl.BlockSpec((nbatch, steps, ATT_WIDTH), lambda i: (0, i, 0)),
                  row(512)]
        + [gate(k) for k in range(2 * N_BRANCH)]
        + [
            pl.BlockSpec((N_BRANCH, BRANCH_WIDTH, D_MODEL), lambda i: (0, 0, 0)),
            pl.BlockSpec((D_MODEL, D_MODEL), lambda i: (0, 0)),
            pl.BlockSpec((1, D_MODEL), lambda i: (0, 0)),
            pl.BlockSpec((1, D_MODEL), lambda i: (0, 0)),
        ],
        out_specs=row(D_MODEL),
        out_shape=jax.ShapeDtypeStruct((T, D_MODEL), F32),
        scratch_shapes=[pltpu.VMEM((ATT_WIDTH // LANES, tm, LANES), F32)],
        compiler_params=_cparams("arbitrary"),
        name="merge_ln",
    )(h, y_a, y_b, y_c, z, z, z, z, z, z, w_branch, w_out, ln_g, ln_b)


def _ffn_kernel(x_ref, w1_ref, w3_ref, w2_ref, lng_ref, lnb_ref, o_ref, xb_ref, acc_ref):
    f = pl.program_id(1)

    @pl.when(f == 0)
    def _():
        xb_ref[...] = x_ref[...].astype(BF16)
        acc_ref[...] = jnp.zeros_like(acc_ref)

    xb = xb_ref[...]
    a = jax.nn.silu(jnp.dot(xb, w1_ref[...], preferred_element_type=F32))
    a = a * jnp.dot(xb, w3_ref[...], preferred_element_type=F32)
    acc_ref[...] += jnp.dot(a.astype(BF16), w2_ref[...], preferred_element_type=F32)

    @pl.when(f == pl.num_programs(1) - 1)
    def _():
        r = DN_ALPHA * x_ref[...] + acc_ref[...]
        o_ref[...] = _layer_norm(r, lng_ref[...], lnb_ref[...])


FFN_TF = 1408


def _ffn(x, w1, w3, w2, ln_g, ln_b):
    T = x.shape[0]
    tm, tf = 512, FFN_TF
    return pl.pallas_call(
        _ffn_kernel,
        grid=(T // tm, D_FF // tf),
        in_specs=[
            pl.BlockSpec((tm, D_MODEL), lambda i, f: (i, 0)),
            pl.BlockSpec((D_MODEL, tf), lambda i, f: (0, f)),
            pl.BlockSpec((D_MODEL, tf), lambda i, f: (0, f)),
            pl.BlockSpec((tf, D_MODEL), lambda i, f: (f, 0)),
            pl.BlockSpec((1, D_MODEL), lambda i, f: (0, 0)),
            pl.BlockSpec((1, D_MODEL), lambda i, f: (0, 0)),
        ],
        out_specs=pl.BlockSpec((tm, D_MODEL), lambda i, f: (i, 0)),
        out_shape=jax.ShapeDtypeStruct((T, D_MODEL), F32),
        scratch_shapes=[pltpu.VMEM((tm, D_MODEL), BF16), pltpu.VMEM((tm, D_MODEL), F32)],
        compiler_params=_cparams("arbitrary", "arbitrary"),
        name="ffn_ln",
    )(x, w1, w3, w2, ln_g, ln_b)


def _router_kernel(x_ref, w_ref, b_ref, idx_ref, wgt_ref):
    logits = jnp.dot(x_ref[...], w_ref[...], preferred_element_type=F32,
                     precision=lax.Precision.HIGHEST) + b_ref[...]
    lane = lax.broadcasted_iota(jnp.int32, logits.shape, 1)
    l1 = jnp.where(lane < N_EXPERTS, logits, -jnp.inf)
    m1 = jnp.max(l1, axis=1, keepdims=True)
    i1 = jnp.min(jnp.where(l1 == m1, lane, LANES), axis=1, keepdims=True)
    l2 = jnp.where(lane == i1, -jnp.inf, l1)
    m2 = jnp.max(l2, axis=1, keepdims=True)
    i2 = jnp.min(jnp.where(l2 == m2, lane, LANES), axis=1, keepdims=True)
    e = jnp.exp(m2 - m1)
    w_top = 1.0 / (1.0 + e)
    w_sec = e / (1.0 + e)
    idx_ref[...] = jnp.where(lane == 0, i1, jnp.where(lane == 1, i2, 0))
    wgt_ref[...] = jnp.where(lane == 0, w_top, jnp.where(lane == 1, w_sec, 0.0))


def _router(x, w, b):
    T = x.shape[0]
    tm = 1024
    out = pl.BlockSpec((tm, LANES), lambda i: (i, 0))
    return pl.pallas_call(
        _router_kernel,
        grid=(T // tm,),
        in_specs=[
            pl.BlockSpec((tm, D_MODEL), lambda i: (i, 0)),
            pl.BlockSpec((D_MODEL, LANES), lambda i: (0, 0)),
            pl.BlockSpec((1, LANES), lambda i: (0, 0)),
        ],
        out_specs=[out, out],
        out_shape=[jax.ShapeDtypeStruct((T, LANES), jnp.int32),
                   jax.ShapeDtypeStruct((T, LANES), F32)],
        compiler_params=_cparams("arbitrary"),
        name="moe_router",
    )(x, w, b)


MOE_TM = 512
MOE_DMA_UNROLL = 8


def _route_plan(idx, wgt, T):
    n_assign = TOP_K * T
    n_tiles = n_assign // MOE_TM + N_EXPERTS
    e_flat = idx.T.reshape(-1)
    w_flat = wgt.T.reshape(-1)
    order = jnp.argsort(e_flat, stable=True).astype(jnp.int32)
    counts = jnp.sum((e_flat[:, None] == jnp.arange(N_EXPERTS)[None, :]).astype(jnp.int32), axis=0)
    group_start = jnp.cumsum(counts) - counts
    tiles_per = (counts + MOE_TM - 1) // MOE_TM
    tile_end = jnp.cumsum(tiles_per)
    tile_start = tile_end - tiles_per
    tile_id = jnp.arange(n_tiles, dtype=jnp.int32)
    used = tile_id < tile_end[-1]
    tile_expert = jnp.minimum(jnp.sum((tile_id[:, None] >= tile_end[None, :]).astype(jnp.int32), axis=1),
                              N_EXPERTS - 1)
    tile_first = (tile_id - tile_start[tile_expert]) * MOE_TM
    tile_rows = jnp.where(used, jnp.clip(counts[tile_expert] - tile_first, 0, MOE_TM), 0)
    last_expert = tile_expert[jnp.maximum(tile_end[-1] - 1, 0)]
    tile_expert = jnp.where(used, tile_expert, last_expert).astype(jnp.int32)
    j = jnp.arange(MOE_TM, dtype=jnp.int32)[None, :]
    valid = j < tile_rows[:, None]
    pos = jnp.clip(group_start[tile_expert][:, None] + tile_first[:, None] + j, 0, n_assign - 1)
    a = jnp.where(valid, order[pos], 0)
    src_tok = jnp.where(a >= T, a - T, a).astype(jnp.int32)
    row_w = jnp.where(valid, w_flat[a], 0.0)
    return (tile_expert, tile_rows.astype(jnp.int32), src_tok[:, None, :], a.astype(jnp.int32)[:, None, :],
            row_w[:, :, None])


def _moe_kernel(te_ref, tn_ref, src_ref, dst_ref, roww_ref, x_hbm, w1_ref, w3_ref, w2_ref, buf_hbm,
                xg, xb, acc, y_scr, gsem, ssem):
    t = pl.program_id(0)
    f = pl.program_id(1)
    nrows = tn_ref[t]

    def start_gather(r):
        pltpu.make_async_copy(x_hbm.at[pl.ds(src_ref[0, r], 1), :], xg.at[pl.ds(r, 1), :], gsem).start()

    def start_scatter(r):
        pltpu.make_async_copy(y_scr.at[pl.ds(r, 1), :], buf_hbm.at[pl.ds(dst_ref[0, r], 1), :], ssem).start()

    def for_rows(fn):
        ngroups = nrows // MOE_DMA_UNROLL

        def group(g, c):
            base = pl.multiple_of(g * MOE_DMA_UNROLL, MOE_DMA_UNROLL)
            for u in range(MOE_DMA_UNROLL):
                fn(base + u)
            return c

        def single(r, c):
            fn(r)
            return c

        lax.fori_loop(0, ngroups, group, 0)
        lax.fori_loop(ngroups * MOE_DMA_UNROLL, nrows, single, 0)

    def wait_rows(src, dst, sem):
        p = MOE_TM
        while p >= 1:
            @pl.when((nrows & p) != 0)
            def _(p=p):
                pltpu.make_async_copy(src.at[pl.ds(0, p), :], dst.at[pl.ds(0, p), :], sem).wait()
            p //= 2

    @pl.when((t == 0) & (f == 0))
    def _():
        xg[...] = jnp.zeros_like(xg)

    @pl.when((f == 0) & (nrows > 0))
    def _():
        for_rows(start_gather)
        wait_rows(x_hbm, xg, gsem)
        xb[...] = xg[...].astype(BF16)

    @pl.when(nrows > 0)
    def _():
        x = xb[...]
        a = jax.nn.silu(jnp.dot(x, w1_ref[...], preferred_element_type=F32))
        a = a * jnp.dot(x, w3_ref[...], preferred_element_type=F32)
        y = jnp.dot(a.astype(BF16), w2_ref[...], preferred_element_type=F32)

        @pl.when(f == 0)
        def _():
            acc[...] = y

        @pl.when(f > 0)
        def _():
            acc[...] += y

    @pl.when((f == pl.num_programs(1) - 1) & (nrows > 0))
    def _():
        y_scr[...] = acc[...] * roww_ref[...]
        for_rows(start_scatter)
        wait_rows(y_scr, buf_hbm, ssem)


def _moe_experts(x, plan, w1, w3, w2):
    T = x.shape[0]
    tile_expert, tile_rows, src_tok, dst_row, row_w = plan
    n_tiles = tile_expert.shape[0]
    tf = FFN_TF
    smem_rows = pl.BlockSpec((None, 1, MOE_TM), lambda t, f, te, tn: (t, 0, 0), memory_space=pltpu.SMEM)
    grid_spec = pltpu.PrefetchScalarGridSpec(
        num_scalar_prefetch=2,
        grid=(n_tiles, D_FF // tf),
        in_specs=[
            smem_rows,
            smem_rows,
            pl.BlockSpec((None, MOE_TM, 1), lambda t, f, te, tn: (t, 0, 0)),
            pl.BlockSpec(memory_space=pl.ANY),
            pl.BlockSpec((None, D_MODEL, tf), lambda t, f, te, tn: (te[t], 0, f)),
            pl.BlockSpec((None, D_MODEL, tf), lambda t, f, te, tn: (te[t], 0, f)),
            pl.BlockSpec((None, tf, D_MODEL), lambda t, f, te, tn: (te[t], f, 0)),
        ],
        out_specs=pl.BlockSpec(memory_space=pl.ANY),
        scratch_shapes=[
            pltpu.VMEM((MOE_TM, D_MODEL), F32),
            pltpu.VMEM((MOE_TM, D_MODEL), BF16),
            pltpu.VMEM((MOE_TM, D_MODEL), F32),
            pltpu.VMEM((MOE_TM, D_MODEL), F32),
            pltpu.SemaphoreType.DMA(()),
            pltpu.SemaphoreType.DMA(()),
        ],
    )
    return pl.pallas_call(
        _moe_kernel,
        grid_spec=grid_spec,
        out_shape=jax.ShapeDtypeStruct((TOP_K * T, D_MODEL), F32),
        compiler_params=_cparams("arbitrary", "arbitrary"),
        name="moe_experts",
    )(tile_expert, tile_rows, src_tok, dst_row, row_w, x, w1, w3, w2)


def _moe_combine_kernel(x_ref, y0_ref, y1_ref, lng_ref, lnb_ref, o_ref):
    r = DN_ALPHA * x_ref[...] + (y0_ref[...] + y1_ref[...])
    o_ref[...] = _layer_norm(r, lng_ref[...], lnb_ref[...])


def _moe_combine(x, buf, ln_g, ln_b):
    T = x.shape[0]
    tm = 1024
    nblk = T // tm
    return pl.pallas_call(
        _moe_combine_kernel,
        grid=(nblk,),
        in_specs=[
            pl.BlockSpec((tm, D_MODEL), lambda i: (i, 0)),
            pl.BlockSpec((tm, D_MODEL), lambda i: (i, 0)),
            pl.BlockSpec((tm, D_MODEL), lambda i: (i + nblk, 0)),
            pl.BlockSpec((1, D_MODEL), lambda i: (0, 0)),
            pl.BlockSpec((1, D_MODEL), lambda i: (0, 0)),
        ],
        out_specs=pl.BlockSpec((tm, D_MODEL), lambda i: (i, 0)),
        out_shape=jax.ShapeDtypeStruct((T, D_MODEL), F32),
        compiler_params=_cparams("arbitrary"),
        name="moe_combine_ln",
    )(x, buf, buf, ln_g, ln_b)


def kernel(x, ln1_g, ln1_b, w_in, b_in, w_pool, pool_scale, ssm_a_re, ssm_a_im, ssm_log_dt,
           ssm_b_re, ssm_b_im, ssm_c_re, ssm_c_im, ssm_d, w_glu, b_glu, w_branch, w_out,
           ln2_g, ln2_b, ffn_w1, ffn_w3, ffn_w2, moe_router, moe_router_b, moe_w1, moe_w3, moe_w2):
    B, S, D = x.shape
    T = B * S
    assert B == SUBLANES and D == D_MODEL
    h = x.transpose(1, 0, 2).reshape(T, D)

    lr, li, bbr, bbi = _s5_prep(ssm_a_re, ssm_a_im, ssm_log_dt, ssm_b_re, ssm_b_im)
    row = lambda a: a.reshape(1, -1)

    for i in range(DEPTH):
        qkv, z = _inproj(h, w_in[i].astype(BF16), row(b_in[i]), B)
        y_b = _attention(qkv)
        y_a = _pool(z, w_pool[i].astype(BF16), row(pool_scale[i]), B)
        b_mat, c_mat, lam = _s5_layout(lr[i], li[i], bbr[i], bbi[i], ssm_c_re[i], ssm_c_im[i], B)
        y_c = _s5(z, b_mat, lam, c_mat, row(ssm_d[i]), w_glu[i].astype(BF16), row(b_glu[i]), B)
        h = _merge(h, y_a, y_b, y_c, z, w_branch[i].astype(BF16), w_out[i].astype(BF16),
                   row(ln1_g[i]), row(ln1_b[i]))
        j = i // 2
        if i % 2 == 0:
            h = _ffn(h, ffn_w1[j].astype(BF16), ffn_w3[j].astype(BF16), ffn_w2[j].astype(BF16),
                     row(ln2_g[i]), row(ln2_b[i]))
        else:
            rw = jnp.pad(moe_router[j], ((0, 0), (0, LANES - N_EXPERTS)))
            rb = jnp.pad(moe_router_b[j], (0, LANES - N_EXPERTS)).reshape(1, LANES)
            idx, wgt = _router(h, rw, rb)
            plan = _route_plan(idx[:, :TOP_K], wgt[:, :TOP_K], T)
            buf = _moe_experts(h, plan, moe_w1[j].astype(BF16), moe_w3[j].astype(BF16),
                               moe_w2[j].astype(BF16))
            h = _moe_combine(h, buf, row(ln2_g[i]), row(ln2_b[i]))
    return h.reshape(S, B, D).transpose(1, 0, 2)
```

```python
import functools

import jax
import jax.numpy as jnp
from jax import lax
from jax.experimental import pallas as pl
from jax.experimental.pallas import tpu as pltpu

F32 = jnp.float32
BF16 = jnp.bfloat16

D_MODEL = 1024
DEPTH = 4
POOL_WINDOWS = (2, 4, 8, 16)
POOL_WIDTH = 512
POOL_GROUP = POOL_WIDTH // len(POOL_WINDOWS)
ATT_HEADS = 8
HEAD_DIM = 64
ATT_WIDTH = ATT_HEADS * HEAD_DIM
MOBA_BLOCK = 256
MOBA_TOPK = 3
SSM_WIDTH = 512
SSM_GROUP = 16
SSM_GROUPS = SSM_WIDTH // SSM_GROUP
SSM_STATE = 64
N_BRANCH = 3
BRANCH_WIDTH = 512
IN_WIDTH = POOL_WIDTH + 3 * ATT_WIDTH + SSM_WIDTH + N_BRANCH * D_MODEL
D_FF = 2816
N_EXPERTS = 8
TOP_K = 2
DN_ALPHA = (2 * DEPTH) ** 0.25
LN_EPS = 1e-5

LANES = 128
SUBLANES = 8
VMEM_LIMIT = 48 * 1024 * 1024

NEG = -1e30

COL_POOL, COL_SSM, COL_GATE = 0, 1, 2
QKV_WIDTH = 3 * ATT_WIDTH
Z_WIDTH = IN_WIDTH - QKV_WIDTH
N_QKV_BLOCKS = QKV_WIDTH // 512


def _cparams(*sem):
    return pltpu.CompilerParams(dimension_semantics=sem, vmem_limit_bytes=VMEM_LIMIT)


def _layer_norm(r, g, b):
    mu = jnp.mean(r, axis=-1, keepdims=True)
    c = r - mu
    var = jnp.mean(c * c, axis=-1, keepdims=True)
    return c * lax.rsqrt(var + LN_EPS) * g + b


CAST_BLOCK_BYTES = 6 * 1024 * 1024


def _cast_kernel(w_ref, o_ref):
    o_ref[...] = w_ref[...].astype(o_ref.dtype)


def _to_bf16(w, lead):
    *outer, R, C = w.shape
    free = outer[len(lead):]
    rb = R
    while rb * C * 4 > CAST_BLOCK_BYTES and rb % 2 == 0 and (rb // 2) % SUBLANES == 0:
        rb //= 2
    nfree = len(free)
    squeeze = (None,) * len(outer)
    in_map = lambda *g: (*lead, *g[:nfree], g[nfree], 0)
    out_map = lambda *g: (*g[:nfree], g[nfree], 0)
    return pl.pallas_call(
        _cast_kernel,
        grid=(*free, R // rb),
        in_specs=[pl.BlockSpec((*squeeze, rb, C), in_map)],
        out_specs=pl.BlockSpec(((None,) * nfree) + (rb, C), out_map),
        out_shape=jax.ShapeDtypeStruct((*free, R, C), BF16),
        compiler_params=_cparams(*(("arbitrary",) * (nfree + 1))),
        name="cast_bf16",
    )(w)


def _inproj_kernel(x_ref, w_ref, b_ref, qkv_ref, z_ref, xb_ref, acc_scr, *, nbatch):
    j = pl.program_id(1)

    @pl.when(j == 0)
    def _():
        xb_ref[...] = x_ref[...].astype(BF16)

    acc = jnp.dot(xb_ref[...], w_ref[...], preferred_element_type=F32) + b_ref[...]

    @pl.when(j < N_QKV_BLOCKS)
    def _():
        nslab, rows, _ = acc_scr.shape
        steps = rows // nbatch
        for c in range(nslab):
            acc_scr[c] = acc[:, c * LANES:(c + 1) * LANES]
        for b in range(nbatch):
            for c in range(nslab):
                qkv_ref[b, :, c * LANES:(c + 1) * LANES] = (
                    acc_scr[c, pl.ds(b, steps, stride=nbatch), :].astype(qkv_ref.dtype))

    @pl.when(j >= N_QKV_BLOCKS)
    def _():
        z_ref[...] = acc.astype(z_ref.dtype)


def _qkv_first(j):
    return jnp.where(j < N_QKV_BLOCKS, j + 1, jnp.where(j == N_QKV_BLOCKS, 0, j))


def _inproj(x, w, b, nbatch):
    T, K = x.shape
    N = w.shape[1]
    S = T // nbatch
    tm, tn = 2048, 512
    steps = tm // nbatch
    kern = functools.partial(_inproj_kernel, nbatch=nbatch)
    return pl.pallas_call(
        kern,
        grid=(T // tm, N // tn),
        in_specs=[
            pl.BlockSpec((tm, K), lambda i, j: (i, 0)),
            pl.BlockSpec((K, tn), lambda i, j: (0, _qkv_first(j))),
            pl.BlockSpec((1, tn), lambda i, j: (0, _qkv_first(j))),
        ],
        out_specs=[
            pl.BlockSpec((nbatch, steps, tn), lambda i, j: (0, i, jnp.minimum(j, N_QKV_BLOCKS - 1))),
            pl.BlockSpec((tm, tn), lambda i, j: (i, jnp.maximum(j - N_QKV_BLOCKS, 0))),
        ],
        out_shape=[jax.ShapeDtypeStruct((nbatch, S, QKV_WIDTH), BF16),
                   jax.ShapeDtypeStruct((T, Z_WIDTH), BF16)],
        scratch_shapes=[pltpu.VMEM((tm, K), BF16), pltpu.VMEM((tn // LANES, tm, LANES), F32)],
        compiler_params=_cparams("arbitrary", "arbitrary"),
        name="inproj",
    )(x, w, b)


def _attn_block(i, q_ref, k_ref, v_ref, o_ref, kmh_ref, kml_ref, kaug_ref, s_scr, p_scr,
                *, nb, blk, dh, n_heads, topk):
    pair = pl.program_id(1)
    W = 2 * dh
    lane = lax.broadcasted_iota(jnp.int32, (1, W), 1)
    head_of_lane = lane // dh
    nt = (((1,), (1,)), ((), ()))

    if i == 0:
        kmh_ref[...] = jnp.zeros_like(kmh_ref)
        kml_ref[...] = jnp.zeros_like(kml_ref)
        for n in range(nb):
            kb = k_ref[n * blk:(n + 1) * blk, :].astype(F32)
            mean = jnp.sum(kb, axis=0, keepdims=True) * (1.0 / blk)
            for hl in range(2):
                row = jnp.where(head_of_lane == hl, mean, 0.0)
                hi = row.astype(BF16)
                lo = (row - hi.astype(F32)).astype(BF16)
                kmh_ref[hl * nb + n:hl * nb + n + 1, :] = hi
                kml_ref[hl * nb + n:hl * nb + n + 1, :] = lo

    par = i % 2
    q = q_ref[i * blk:(i + 1) * blk, :] * jnp.asarray(dh ** -0.5, BF16)
    slopes = []
    for hl in range(2):
        h_f = (pair * 2 + hl + 1).astype(F32)
        slopes.append(jnp.exp2(jnp.full((1, blk), -8.0 / n_heads, F32) * h_f))

    ones = jnp.where((lane == 0) | (lane == dh), 1.0, 0.0)
    if i > 0:
        gate_t = (lax.dot_general(kmh_ref[...], q, nt, preferred_element_type=F32)
                  + lax.dot_general(kml_ref[...], q, nt, preferred_element_type=F32))
        row = lax.broadcasted_iota(jnp.int32, (nb, blk), 0)
        rowdist = ((row - i) * blk).astype(F32)
        parts = []
        for hl in range(2):
            g = gate_t[hl * nb:(hl + 1) * nb, :]
            cnt = jnp.zeros((nb, blk), F32)
            for n2 in range(i):
                r = g[n2:n2 + 1, :]
                cnt = cnt + jnp.where(r > g, 1.0,
                                      jnp.where(r == g, jnp.where(n2 < row, 1.0, 0.0), 0.0))
            sel = jnp.where(cnt < float(topk), 1.0, 0.0)
            parts.append(jnp.where(row < i, jnp.where(sel > 0.5, slopes[hl] * rowdist, NEG), 0.0))
        zrows = lambda n: jnp.zeros((n, blk), F32)
        bias_t = jnp.concatenate([zrows(nb), parts[1], zrows(dh - 2 * nb), zrows(nb), parts[0],
                                  zrows(dh - 2 * nb)], axis=0)
        aug = (bias_t.T + ones).astype(BF16)
    else:
        aug = jnp.broadcast_to(ones, (blk, W)).astype(BF16)

    r_io = lax.broadcasted_iota(jnp.int32, (blk, blk), 0)
    c_io = lax.broadcasted_iota(jnp.int32, (blk, blk), 1)
    nkeys = (i + 1) * blk

    outs = []
    for hl in range(2):
        qh = jnp.where(head_of_lane == hl, q, aug)
        macc = None
        for n in range(i + 1):
            s = lax.dot_general(qh, kaug_ref[hl, n * blk:(n + 1) * blk, :], nt,
                                preferred_element_type=F32)
            if n == i:
                s = jnp.where(c_io <= r_io, s, NEG)
            s_scr[par, hl, :, n * blk:(n + 1) * blk] = s
            t = jnp.maximum(s[:, :LANES], s[:, LANES:])
            macc = t if macc is None else jnp.maximum(macc, t)
        m = jnp.max(macc, axis=1, keepdims=True)
        lacc = None
        for n in range(i + 1):
            p = jnp.exp(s_scr[par, hl, :, n * blk:(n + 1) * blk] - m)
            p_scr[par, hl, :, n * blk:(n + 1) * blk] = p.astype(BF16)
            t = p[:, :LANES] + p[:, LANES:]
            lacc = t if lacc is None else lacc + t
        l = jnp.sum(lacc, axis=1, keepdims=True)
        acc = jnp.dot(p_scr[par, hl, :, 0:nkeys], v_ref[0:nkeys, :], preferred_element_type=F32)
        outs.append(acc * (1.0 / l))

    o_ref[i * blk:(i + 1) * blk, :] = (
        jnp.where(head_of_lane == 0, outs[0], outs[1]).astype(o_ref.dtype))


def _attn_aug_keys(k_ref, kaug_ref, *, nb, blk, dh, n_heads):
    pair = pl.program_id(1)
    lane = lax.broadcasted_iota(jnp.int32, (1, 2 * dh), 1)
    key_pos = lax.broadcasted_iota(jnp.int32, (blk, 1), 0).astype(F32)
    for hl in range(2):
        h_f = (pair * 2 + hl + 1).astype(F32)
        slope = jnp.exp2(jnp.full((blk, 1), -8.0 / n_heads, F32) * h_f)
        base = (1 - hl) * dh
        for n in range(nb):
            extra = jnp.where(lane == base, slope * key_pos,
                              jnp.where(lane == base + nb + n, 1.0, 0.0))
            kaug_ref[hl, n * blk:(n + 1) * blk, :] = jnp.where(
                lane // dh == hl, k_ref[n * blk:(n + 1) * blk, :], extra.astype(BF16))


def _attn_kernel(q_ref, k_ref, v_ref, o_ref, kmh_ref, kml_ref, kaug_ref, s_scr, p_scr,
                 *, nb, blk, dh, n_heads, topk):
    _attn_aug_keys(k_ref, kaug_ref, nb=nb, blk=blk, dh=dh, n_heads=n_heads)
    for i in range(nb):
        _attn_block(i, q_ref, k_ref, v_ref, o_ref, kmh_ref, kml_ref, kaug_ref, s_scr, p_scr,
                    nb=nb, blk=blk, dh=dh, n_heads=n_heads, topk=topk)


def _attention(qkv):
    B, S, _ = qkv.shape
    blk, dh = MOBA_BLOCK, HEAD_DIM
    nb = S // blk
    assert nb * blk == S and nb == SUBLANES and 2 * dh == LANES
    assert 8 % ATT_HEADS == 0 and 2 * nb + 1 <= dh
    n_pairs = ATT_HEADS // 2
    kern = functools.partial(_attn_kernel, nb=nb, blk=blk, dh=dh, n_heads=ATT_HEADS, topk=MOBA_TOPK)
    return pl.pallas_call(
        kern,
        grid=(B, n_pairs),
        in_specs=[
            pl.BlockSpec((None, S, LANES), lambda b, p: (b, 0, p)),
            pl.BlockSpec((None, S, LANES), lambda b, p: (b, 0, n_pairs + p)),
            pl.BlockSpec((None, S, LANES), lambda b, p: (b, 0, 2 * n_pairs + p)),
        ],
        out_specs=pl.BlockSpec((None, S, LANES), lambda b, p: (b, 0, p)),
        out_shape=jax.ShapeDtypeStruct((B, S, ATT_WIDTH), F32),
        scratch_shapes=[
            pltpu.VMEM((LANES, LANES), BF16),
            pltpu.VMEM((LANES, LANES), BF16),
            pltpu.VMEM((2, S, LANES), BF16),
            pltpu.VMEM((2, 2, blk, nb * blk), F32),
            pltpu.VMEM((2, 2, blk, nb * blk), BF16),
        ],
        compiler_params=_cparams("arbitrary", "arbitrary"),
        name="moba_attn",
    )(qkv, qkv, qkv)


POOL_HALO_STEPS = max(POOL_WINDOWS)


def _pool_kernel(u_ref, w_ref, scale_ref, o_ref, ext_scr, *, steps, nbatch):
    c = pl.program_id(0)
    R = steps * nbatch
    H = POOL_HALO_STEPS * nbatch

    @pl.when(c == 0)
    def _():
        ext_scr[0:H, :] = jnp.zeros((H, POOL_WIDTH), F32)

    @pl.when(c > 0)
    def _():
        ext_scr[0:H, :] = ext_scr[R:R + H, :]

    ext_scr[H:H + R, :] = u_ref[...].astype(F32)

    t_glob = c * steps + lax.broadcasted_iota(jnp.int32, (R, 1), 0) // nbatch
    outs = []
    for gi, w in enumerate(POOL_WINDOWS):
        cols = slice(gi * POOL_GROUP, (gi + 1) * POOL_GROUP)
        u = ext_scr[H:H + R, cols]
        acc = u
        for k in range(1, w):
            acc = acc + ext_scr[H - k * nbatch:H - k * nbatch + R, cols]
        count = jnp.minimum(t_glob + 1, w).astype(F32)
        d = acc / count - u
        outs.append(jnp.dot(d.astype(BF16), w_ref[gi], preferred_element_type=F32))
    y = jnp.concatenate(outs, axis=1) * scale_ref[...]
    o_ref[...] = y.astype(o_ref.dtype)


def _pool(z, w_pool, pool_scale, nbatch):
    T = z.shape[0]
    steps = 64
    R = steps * nbatch
    H = POOL_HALO_STEPS * nbatch
    kern = functools.partial(_pool_kernel, steps=steps, nbatch=nbatch)
    return pl.pallas_call(
        kern,
        grid=(T // R,),
        in_specs=[
            pl.BlockSpec((R, 512), lambda c: (c, COL_POOL)),
            pl.BlockSpec((len(POOL_WINDOWS), POOL_GROUP, POOL_GROUP), lambda c: (0, 0, 0)),
            pl.BlockSpec((1, POOL_WIDTH), lambda c: (0, 0)),
        ],
        out_specs=pl.BlockSpec((R, POOL_WIDTH), lambda c: (c, 0)),
        out_shape=jax.ShapeDtypeStruct((T, POOL_WIDTH), BF16),
        scratch_shapes=[pltpu.VMEM((H + R, POOL_WIDTH), F32)],
        compiler_params=_cparams("arbitrary"),
        name="pool_mixer",
    )(z, w_pool, pool_scale)


def _s5_prep_kernel(ar_ref, ai_ref, ldt_ref, br_ref, bi_ref, lr_ref, li_ref, bbr_ref, bbi_ref):
    ar, ai = ar_ref[...], ai_ref[...]
    dt = jnp.exp(ldt_ref[...])
    mag = jnp.exp(ar * dt)
    lr = mag * jnp.cos(ai * dt)
    li = mag * jnp.sin(ai * dt)
    den = ar * ar + ai * ai
    nr = lr - 1.0
    fr = (nr * ar + li * ai) / den
    fi = (li * ar - nr * ai) / den
    br, bi = br_ref[...], bi_ref[...]
    lr_ref[...] = lr
    li_ref[...] = li
    bbr_ref[...] = fr * br - fi * bi
    bbi_ref[...] = fr * bi + fi * br


def _s5_prep(a_re, a_im, log_dt, b_re, b_im):
    L, G, N = a_re.shape
    P = b_re.shape[-1]
    rows, cols = L * G, N * P
    exp_n = lambda a: jnp.broadcast_to(a[..., None], (L, G, N, P)).reshape(rows, cols)
    ldt = jnp.broadcast_to(log_dt[..., None, None], (L, G, N, P)).reshape(rows, cols)
    shp = jax.ShapeDtypeStruct((rows, cols), F32)
    spec = pl.BlockSpec((rows, cols), lambda: (0, 0))
    lr, li, bbr, bbi = pl.pallas_call(
        _s5_prep_kernel,
        in_specs=[spec] * 5,
        out_specs=[spec] * 4,
        out_shape=[shp] * 4,
        name="s5_discretise",
    )(exp_n(a_re), exp_n(a_im), ldt, b_re.reshape(rows, cols), b_im.reshape(rows, cols))
    lr = lr.reshape(L, G, N, P)[..., 0].reshape(L, G * N)
    li = li.reshape(L, G, N, P)[..., 0].reshape(L, G * N)
    return lr, li, bbr.reshape(L, G, N, P), bbi.reshape(L, G, N, P)


S5_GPB = LANES // SSM_GROUP
S5_NBLK = SSM_WIDTH // LANES
S5_SB = S5_GPB * SSM_STATE
S5_NS = SSM_GROUPS * SSM_STATE


def _s5_layout(lr, li, bbr, bbi, c_re, c_im, nbatch):
    eye = jnp.eye(S5_GPB, dtype=F32)

    def bmat(bb):
        bb = bb.reshape(S5_NBLK, S5_GPB, SSM_STATE, SSM_GROUP)
        m = jnp.einsum('jgnp,gh->jgphn', bb, eye)
        return m.reshape(S5_NBLK, LANES, S5_SB)

    def cmat(c):
        c = c.reshape(S5_NBLK, S5_GPB, SSM_GROUP, SSM_STATE)
        m = jnp.einsum('jgpn,gh->jgnhp', c, eye)
        return m.reshape(S5_NBLK, S5_SB, LANES)

    b_mat = jnp.concatenate([bmat(bbr), bmat(bbi)], axis=2).astype(BF16)
    c_mat = jnp.concatenate([cmat(c_re), -cmat(c_im)], axis=1).astype(BF16)
    lam = jnp.concatenate([jnp.broadcast_to(lr[None, :], (nbatch, S5_NS)),
                           jnp.broadcast_to(li[None, :], (nbatch, S5_NS))], axis=0)
    return b_mat, c_mat, lam


S5_SCAN_COLS = 1024


def _s5_kernel(u_ref, bmat_ref, lam_ref, cmat_ref, d_ref, wglu_ref, bglu_ref, o_ref,
               x_scr, h_scr, *, steps, nbatch):
    @pl.when(pl.program_id(0) == 0)
    def _():
        h_scr[...] = jnp.zeros_like(h_scr)

    u = u_ref[...]
    for j in range(S5_NBLK):
        xj = jnp.dot(u[:, j * LANES:(j + 1) * LANES], bmat_ref[j], preferred_element_type=F32)
        x_scr[:, j * S5_SB:(j + 1) * S5_SB] = xj[:, :S5_SB]
        x_scr[:, S5_NS + j * S5_SB:S5_NS + (j + 1) * S5_SB] = xj[:, S5_SB:]

    for c in range(S5_NS // S5_SCAN_COLS):
        re = slice(c * S5_SCAN_COLS, (c + 1) * S5_SCAN_COLS)
        im = slice(S5_NS + c * S5_SCAN_COLS, S5_NS + (c + 1) * S5_SCAN_COLS)
        lr = lam_ref[0:nbatch, re]
        li = lam_ref[nbatch:2 * nbatch, re]

        def body(t, carry, re=re, im=im, lr=lr, li=li):
            hr, hi = carry
            r0 = pl.multiple_of(t * nbatch, nbatch)
            nhr = lr * hr - li * hi + x_scr[pl.ds(r0, nbatch), re]
            nhi = lr * hi + li * hr + x_scr[pl.ds(r0, nbatch), im]
            x_scr[pl.ds(r0, nbatch), re] = nhr
            x_scr[pl.ds(r0, nbatch), im] = nhi
            return nhr, nhi

        hr, hi = lax.fori_loop(0, steps, body, (h_scr[:, re], h_scr[:, im]), unroll=4)
        h_scr[:, re] = hr
        h_scr[:, im] = hi

    ys = []
    for j in range(S5_NBLK):
        hr = x_scr[:, j * S5_SB:(j + 1) * S5_SB].astype(BF16)
        hi = x_scr[:, S5_NS + j * S5_SB:S5_NS + (j + 1) * S5_SB].astype(BF16)
        ys.append(jnp.dot(hr, cmat_ref[j, 0:S5_SB, :], preferred_element_type=F32)
                  + jnp.dot(hi, cmat_ref[j, S5_SB:2 * S5_SB, :], preferred_element_type=F32))
    y = jnp.concatenate(ys, axis=1) + d_ref[...] * u.astype(F32)
    y = jax.nn.gelu(y, approximate=True)
    z = jnp.dot(y.astype(BF16), wglu_ref[...], preferred_element_type=F32) + bglu_ref[...]
    o_ref[...] = (z[:, :SSM_WIDTH] * jax.nn.sigmoid(z[:, SSM_WIDTH:])).astype(o_ref.dtype)


def _s5(z, b_mat, lam, c_mat, d_skip, w_glu, b_glu, nbatch):
    T = z.shape[0]
    steps = 32
    R = steps * nbatch
    kern = functools.partial(_s5_kernel, steps=steps, nbatch=nbatch)
    const = lambda *shape: pl.BlockSpec(shape, lambda c: (0,) * len(shape))
    return pl.pallas_call(
        kern,
        grid=(T // R,),
        in_specs=[
            pl.BlockSpec((R, 512), lambda c: (c, COL_SSM)),
            const(S5_NBLK, LANES, 2 * S5_SB),
            const(2 * nbatch, S5_NS),
            const(S5_NBLK, 2 * S5_SB, LANES),
            const(1, SSM_WIDTH),
            const(SSM_WIDTH, 2 * SSM_WIDTH),
            const(1, 2 * SSM_WIDTH),
        ],
        out_specs=pl.BlockSpec((R, SSM_WIDTH), lambda c: (c, 0)),
        out_shape=jax.ShapeDtypeStruct((T, SSM_WIDTH), BF16),
        scratch_shapes=[pltpu.VMEM((R, 2 * S5_NS), F32), pltpu.VMEM((nbatch, 2 * S5_NS), F32)],
        compiler_params=_cparams("arbitrary"),
        name="s5_mixer",
    )(z, b_mat, lam, c_mat, d_skip, w_glu, b_glu)


def _merge_kernel(h_ref, ya_ref, yb_ref, yc_ref, g0a, g0b, g1a, g1b, g2a, g2b,
                  wbr_ref, wout_ref, lng_ref, lnb_ref, o_ref, yb_scr):
    nbatch, steps, _ = yb_ref.shape
    nslab = yb_scr.shape[0]
    for b in range(nbatch):
        for c in range(nslab):
            yb_scr[c, pl.ds(b, steps, stride=nbatch), :] = yb_ref[b, :, c * LANES:(c + 1) * LANES]
    yb = jnp.concatenate([yb_scr[c] for c in range(nslab)], axis=1).astype(BF16)
    ys = (ya_ref[...], yb, yc_ref[...])
    gs = ((g0a, g0b), (g1a, g1b), (g2a, g2b))
    half = D_MODEL // 2
    merged = []
    for hcol in range(2):
        acc = None
        for n in range(N_BRANCH):
            proj = jnp.dot(ys[n], wbr_ref[n, :, hcol * half:(hcol + 1) * half],
                           preferred_element_type=F32)
            term = jax.nn.sigmoid(gs[n][hcol][...].astype(F32)) * proj
            acc = term if acc is None else acc + term
        merged.append(acc.astype(BF16))
    merged = jnp.concatenate(merged, axis=1)
    mix = jnp.dot(merged, wout_ref[...], preferred_element_type=F32)
    r = DN_ALPHA * h_ref[...] + mix
    o_ref[...] = _layer_norm(r, lng_ref[...], lnb_ref[...])


def _merge(h, y_a, y_b, y_c, z, w_branch, w_out, ln_g, ln_b):
    T = h.shape[0]
    nbatch = y_b.shape[0]
    tm = 512
    steps = tm // nbatch
    row = lambda w: pl.BlockSpec((tm, w), lambda i: (i, 0))
    gate = lambda k: pl.BlockSpec((tm, 512), lambda i, k=k: (i, COL_GATE + k))
    return pl.pallas_call(
        _merge_kernel,
        grid=(T // tm,),
        in_specs=[row(D_MODEL), row(512),
                  pl.BlockSpec((nbatch, steps, ATT_WIDTH), lambda i: (0, i, 0)),
                  row(512)]
        + [gate(k) for k in range(2 * N_BRANCH)]
        + [
            pl.BlockSpec((N_BRANCH, BRANCH_WIDTH, D_MODEL), lambda i: (0, 0, 0)),
            pl.BlockSpec((D_MODEL, D_MODEL), lambda i: (0, 0)),
            pl.BlockSpec((1, D_MODEL), lambda i: (0, 0)),
            pl.BlockSpec((1, D_MODEL), lambda i: (0, 0)),
        ],
        out_specs=row(D_MODEL),
        out_shape=jax.ShapeDtypeStruct((T, D_MODEL), F32),
        scratch_shapes=[pltpu.VMEM((ATT_WIDTH // LANES, tm, LANES), F32)],
        compiler_params=_cparams("arbitrary"),
        name="merge_ln",
    )(h, y_a, y_b, y_c, z, z, z, z, z, z, w_branch, w_out, ln_g, ln_b)


def _ffn_kernel(x_ref, w1_ref, w3_ref, w2_ref, lng_ref, lnb_ref, o_ref, xb_ref, acc_ref):
    f = pl.program_id(1)

    @pl.when(f == 0)
    def _():
        xb_ref[...] = x_ref[...].astype(BF16)
        acc_ref[...] = jnp.zeros_like(acc_ref)

    xb = xb_ref[...]
    a = jax.nn.silu(jnp.dot(xb, w1_ref[...], preferred_element_type=F32))
    a = a * jnp.dot(xb, w3_ref[...], preferred_element_type=F32)
    acc_ref[...] += jnp.dot(a.astype(BF16), w2_ref[...], preferred_element_type=F32)

    @pl.when(f == pl.num_programs(1) - 1)
    def _():
        r = DN_ALPHA * x_ref[...] + acc_ref[...]
        o_ref[...] = _layer_norm(r, lng_ref[...], lnb_ref[...])


FFN_TF = 1408


def _ffn(x, w1, w3, w2, ln_g, ln_b):
    T = x.shape[0]
    tm, tf = 512, FFN_TF
    return pl.pallas_call(
        _ffn_kernel,
        grid=(T // tm, D_FF // tf),
        in_specs=[
            pl.BlockSpec((tm, D_MODEL), lambda i, f: (i, 0)),
            pl.BlockSpec((D_MODEL, tf), lambda i, f: (0, f)),
            pl.BlockSpec((D_MODEL, tf), lambda i, f: (0, f)),
            pl.BlockSpec((tf, D_MODEL), lambda i, f: (f, 0)),
            pl.BlockSpec((1, D_MODEL), lambda i, f: (0, 0)),
            pl.BlockSpec((1, D_MODEL), lambda i, f: (0, 0)),
        ],
        out_specs=pl.BlockSpec((tm, D_MODEL), lambda i, f: (i, 0)),
        out_shape=jax.ShapeDtypeStruct((T, D_MODEL), F32),
        scratch_shapes=[pltpu.VMEM((tm, D_MODEL), BF16), pltpu.VMEM((tm, D_MODEL), F32)],
        compiler_params=_cparams("arbitrary", "arbitrary"),
        name="ffn_ln",
    )(x, w1, w3, w2, ln_g, ln_b)


def _router_kernel(x_ref, w_ref, b_ref, idx_ref, wgt_ref):
    logits = jnp.dot(x_ref[...], w_ref[...], preferred_element_type=F32,
                     precision=lax.Precision.HIGHEST) + b_ref[...]
    lane = lax.broadcasted_iota(jnp.int32, logits.shape, 1)
    l1 = jnp.where(lane < N_EXPERTS, logits, -jnp.inf)
    m1 = jnp.max(l1, axis=1, keepdims=True)
    i1 = jnp.min(jnp.where(l1 == m1, lane, LANES), axis=1, keepdims=True)
    l2 = jnp.where(lane == i1, -jnp.inf, l1)
    m2 = jnp.max(l2, axis=1, keepdims=True)
    i2 = jnp.min(jnp.where(l2 == m2, lane, LANES), axis=1, keepdims=True)
    e = jnp.exp(m2 - m1)
    w_top = 1.0 / (1.0 + e)
    w_sec = e / (1.0 + e)
    idx_ref[...] = jnp.where(lane == 0, i1, jnp.where(lane == 1, i2, 0))
    wgt_ref[...] = jnp.where(lane == 0, w_top, jnp.where(lane == 1, w_sec, 0.0))


def _router(x, w, b):
    T = x.shape[0]
    tm = 1024
    out = pl.BlockSpec((tm, LANES), lambda i: (i, 0))
    return pl.pallas_call(
        _router_kernel,
        grid=(T // tm,),
        in_specs=[
            pl.BlockSpec((tm, D_MODEL), lambda i: (i, 0)),
            pl.BlockSpec((D_MODEL, LANES), lambda i: (0, 0)),
            pl.BlockSpec((1, LANES), lambda i: (0, 0)),
        ],
        out_specs=[out, out],
        out_shape=[jax.ShapeDtypeStruct((T, LANES), jnp.int32),
                   jax.ShapeDtypeStruct((T, LANES), F32)],
        compiler_params=_cparams("arbitrary"),
        name="moe_router",
    )(x, w, b)


MOE_TM = 512
MOE_DMA_UNROLL = 8


def _route_plan(idx, wgt, T):
    n_assign = TOP_K * T
    n_tiles = n_assign // MOE_TM + N_EXPERTS
    e_flat = idx.T.reshape(-1)
    w_flat = wgt.T.reshape(-1)
    order = jnp.argsort(e_flat, stable=True).astype(jnp.int32)
    counts = jnp.sum((e_flat[:, None] == jnp.arange(N_EXPERTS)[None, :]).astype(jnp.int32), axis=0)
    group_start = jnp.cumsum(counts) - counts
    tiles_per = (counts + MOE_TM - 1) // MOE_TM
    tile_end = jnp.cumsum(tiles_per)
    tile_start = tile_end - tiles_per
    tile_id = jnp.arange(n_tiles, dtype=jnp.int32)
    used = tile_id < tile_end[-1]
    tile_expert = jnp.minimum(jnp.sum((tile_id[:, None] >= tile_end[None, :]).astype(jnp.int32), axis=1),
                              N_EXPERTS - 1)
    tile_first = (tile_id - tile_start[tile_expert]) * MOE_TM
    tile_rows = jnp.where(used, jnp.clip(counts[tile_expert] - tile_first, 0, MOE_TM), 0)
    last_expert = tile_expert[jnp.maximum(tile_end[-1] - 1, 0)]
    tile_expert = jnp.where(used, tile_expert, last_expert).astype(jnp.int32)
    j = jnp.arange(MOE_TM, dtype=jnp.int32)[None, :]
    valid = j < tile_rows[:, None]
    pos = jnp.clip(group_start[tile_expert][:, None] + tile_first[:, None] + j, 0, n_assign - 1)
    a = jnp.where(valid, order[pos], 0)
    src_tok = jnp.where(a >= T, a - T, a).astype(jnp.int32)
    row_w = jnp.where(valid, w_flat[a], 0.0)
    return (tile_expert, tile_rows.astype(jnp.int32), src_tok[:, None, :], a.astype(jnp.int32)[:, None, :],
            row_w[:, :, None])


def _moe_kernel(te_ref, tn_ref, src_ref, dst_ref, roww_ref, x_hbm, w1_ref, w3_ref, w2_ref, buf_hbm,
                xg, xb, acc, y_scr, gsem, ssem):
    t = pl.program_id(0)
    f = pl.program_id(1)
    nrows = tn_ref[t]

    def start_gather(r):
        pltpu.make_async_copy(x_hbm.at[pl.ds(src_ref[0, r], 1), :], xg.at[pl.ds(r, 1), :], gsem).start()

    def start_scatter(r):
        pltpu.make_async_copy(y_scr.at[pl.ds(r, 1), :], buf_hbm.at[pl.ds(dst_ref[0, r], 1), :], ssem).start()

    def for_rows(fn):
        ngroups = nrows // MOE_DMA_UNROLL

        def group(g, c):
            base = pl.multiple_of(g * MOE_DMA_UNROLL, MOE_DMA_UNROLL)
            for u in range(MOE_DMA_UNROLL):
                fn(base + u)
            return c

        def single(r, c):
            fn(r)
            return c

        lax.fori_loop(0, ngroups, group, 0)
        lax.fori_loop(ngroups * MOE_DMA_UNROLL, nrows, single, 0)

    def wait_rows(src, dst, sem):
        p = MOE_TM
        while p >= 1:
            @pl.when((nrows & p) != 0)
            def _(p=p):
                pltpu.make_async_copy(src.at[pl.ds(0, p), :], dst.at[pl.ds(0, p), :], sem).wait()
            p //= 2

    @pl.when((t == 0) & (f == 0))
    def _():
        xg[...] = jnp.zeros_like(xg)

    @pl.when((f == 0) & (nrows > 0))
    def _():
        for_rows(start_gather)
        wait_rows(x_hbm, xg, gsem)
        xb[...] = xg[...].astype(BF16)

    @pl.when(nrows > 0)
    def _():
        x = xb[...]
        a = jax.nn.silu(jnp.dot(x, w1_ref[...], preferred_element_type=F32))
        a = a * jnp.dot(x, w3_ref[...], preferred_element_type=F32)
        y = jnp.dot(a.astype(BF16), w2_ref[...], preferred_element_type=F32)

        @pl.when(f == 0)
        def _():
            acc[...] = y

        @pl.when(f > 0)
        def _():
            acc[...] += y

    @pl.when((f == pl.num_programs(1) - 1) & (nrows > 0))
    def _():
        y_scr[...] = acc[...] * roww_ref[...]
        for_rows(start_scatter)
        wait_rows(y_scr, buf_hbm, ssem)


def _moe_experts(x, plan, w1, w3, w2):
    T = x.shape[0]
    tile_expert, tile_rows, src_tok, dst_row, row_w = plan
    n_tiles = tile_expert.shape[0]
    tf = FFN_TF
    smem_rows = pl.BlockSpec((None, 1, MOE_TM), lambda t, f, te, tn: (t, 0, 0), memory_space=pltpu.SMEM)
    grid_spec = pltpu.PrefetchScalarGridSpec(
        num_scalar_prefetch=2,
        grid=(n_tiles, D_FF // tf),
        in_specs=[
            smem_rows,
            smem_rows,
            pl.BlockSpec((None, MOE_TM, 1), lambda t, f, te, tn: (t, 0, 0)),
            pl.BlockSpec(memory_space=pl.ANY),
            pl.BlockSpec((None, D_MODEL, tf), lambda t, f, te, tn: (te[t], 0, f)),
            pl.BlockSpec((None, D_MODEL, tf), lambda t, f, te, tn: (te[t], 0, f)),
            pl.BlockSpec((None, tf, D_MODEL), lambda t, f, te, tn: (te[t], f, 0)),
        ],
        out_specs=pl.BlockSpec(memory_space=pl.ANY),
        scratch_shapes=[
            pltpu.VMEM((MOE_TM, D_MODEL), F32),
            pltpu.VMEM((MOE_TM, D_MODEL), BF16),
            pltpu.VMEM((MOE_TM, D_MODEL), F32),
            pltpu.VMEM((MOE_TM, D_MODEL), F32),
            pltpu.SemaphoreType.DMA(()),
            pltpu.SemaphoreType.DMA(()),
        ],
    )
    return pl.pallas_call(
        _moe_kernel,
        grid_spec=grid_spec,
        out_shape=jax.ShapeDtypeStruct((TOP_K * T, D_MODEL), F32),
        compiler_params=_cparams("arbitrary", "arbitrary"),
        name="moe_experts",
    )(tile_expert, tile_rows, src_tok, dst_row, row_w, x, w1, w3, w2)


def _moe_combine_kernel(x_ref, y0_ref, y1_ref, lng_ref, lnb_ref, o_ref):
    r = DN_ALPHA * x_ref[...] + (y0_ref[...] + y1_ref[...])
    o_ref[...] = _layer_norm(r, lng_ref[...], lnb_ref[...])


def _moe_combine(x, buf, ln_g, ln_b):
    T = x.shape[0]
    tm = 1024
    nblk = T // tm
    return pl.pallas_call(
        _moe_combine_kernel,
        grid=(nblk,),
        in_specs=[
            pl.BlockSpec((tm, D_MODEL), lambda i: (i, 0)),
            pl.BlockSpec((tm, D_MODEL), lambda i: (i, 0)),
            pl.BlockSpec((tm, D_MODEL), lambda i: (i + nblk, 0)),
            pl.BlockSpec((1, D_MODEL), lambda i: (0, 0)),
            pl.BlockSpec((1, D_MODEL), lambda i: (0, 0)),
        ],
        out_specs=pl.BlockSpec((tm, D_MODEL), lambda i: (i, 0)),
        out_shape=jax.ShapeDtypeStruct((T, D_MODEL), F32),
        compiler_params=_cparams("arbitrary"),
        name="moe_combine_ln",
    )(x, buf, buf, ln_g, ln_b)


def kernel(x, ln1_g, ln1_b, w_in, b_in, w_pool, pool_scale, ssm_a_re, ssm_a_im, ssm_log_dt,
           ssm_b_re, ssm_b_im, ssm_c_re, ssm_c_im, ssm_d, w_glu, b_glu, w_branch, w_out,
           ln2_g, ln2_b, ffn_w1, ffn_w3, ffn_w2, moe_router, moe_router_b, moe_w1, moe_w3, moe_w2):
    B, S, D = x.shape
    T = B * S
    assert B == SUBLANES and D == D_MODEL
    h = x.transpose(1, 0, 2).reshape(T, D)

    lr, li, bbr, bbi = _s5_prep(ssm_a_re, ssm_a_im, ssm_log_dt, ssm_b_re, ssm_b_im)
    row = lambda a: a.reshape(1, -1)

    for i in range(DEPTH):
        qkv, z = _inproj(h, _to_bf16(w_in, (i,)), row(b_in[i]), B)
        y_b = _attention(qkv)
        y_a = _pool(z, w_pool[i].astype(BF16), row(pool_scale[i]), B)
        b_mat, c_mat, lam = _s5_layout(lr[i], li[i], bbr[i], bbi[i], ssm_c_re[i], ssm_c_im[i], B)
        y_c = _s5(z, b_mat, lam, c_mat, row(ssm_d[i]), w_glu[i].astype(BF16), row(b_glu[i]), B)
        h = _merge(h, y_a, y_b, y_c, z, w_branch[i].astype(BF16), w_out[i].astype(BF16),
                   row(ln1_g[i]), row(ln1_b[i]))
        j = i // 2
        if i % 2 == 0:
            h = _ffn(h, _to_bf16(ffn_w1, (j,)), _to_bf16(ffn_w3, (j,)), _to_bf16(ffn_w2, (j,)),
                     row(ln2_g[i]), row(ln2_b[i]))
        else:
            rw = jnp.pad(moe_router[j], ((0, 0), (0, LANES - N_EXPERTS)))
            rb = jnp.pad(moe_router_b[j], (0, LANES - N_EXPERTS)).reshape(1, LANES)
            idx, wgt = _router(h, rw, rb)
            plan = _route_plan(idx[:, :TOP_K], wgt[:, :TOP_K], T)
            buf = _moe_experts(h, plan, _to_bf16(moe_w1, (j,)), _to_bf16(moe_w3, (j,)),
                               _to_bf16(moe_w2, (j,)))
            h = _moe_combine(h, buf, row(ln2_g[i]), row(ln2_b[i]))
    return h.reshape(S, B, D).transpose(1, 0, 2)
```

```python
import functools

import jax
import jax.numpy as jnp
from jax import lax
from jax.experimental import pallas as pl
from jax.experimental.pallas import tpu as pltpu

F32 = jnp.float32
BF16 = jnp.bfloat16

D_MODEL = 1024
DEPTH = 4
POOL_WINDOWS = (2, 4, 8, 16)
POOL_WIDTH = 512
POOL_GROUP = POOL_WIDTH // len(POOL_WINDOWS)
ATT_HEADS = 8
HEAD_DIM = 64
ATT_WIDTH = ATT_HEADS * HEAD_DIM
MOBA_BLOCK = 256
MOBA_TOPK = 3
SSM_WIDTH = 512
SSM_GROUP = 16
SSM_GROUPS = SSM_WIDTH // SSM_GROUP
SSM_STATE = 64
N_BRANCH = 3
BRANCH_WIDTH = 512
IN_WIDTH = POOL_WIDTH + 3 * ATT_WIDTH + SSM_WIDTH + N_BRANCH * D_MODEL
D_FF = 2816
N_EXPERTS = 8
TOP_K = 2
DN_ALPHA = (2 * DEPTH) ** 0.25
LN_EPS = 1e-5

LANES = 128
SUBLANES = 8
VMEM_LIMIT = 48 * 1024 * 1024

NEG = -1e30

COL_POOL, COL_SSM, COL_GATE = 0, 1, 2
QKV_WIDTH = 3 * ATT_WIDTH
Z_WIDTH = IN_WIDTH - QKV_WIDTH
N_QKV_BLOCKS = QKV_WIDTH // 512


def _cparams(*sem):
    return pltpu.CompilerParams(dimension_semantics=sem, vmem_limit_bytes=VMEM_LIMIT)


def _layer_norm(r, g, b):
    mu = jnp.mean(r, axis=-1, keepdims=True)
    c = r - mu
    var = jnp.mean(c * c, axis=-1, keepdims=True)
    return c * lax.rsqrt(var + LN_EPS) * g + b


CAST_BLOCK_BYTES = 6 * 1024 * 1024


def _cast_kernel(w_ref, o_ref):
    o_ref[...] = w_ref[...].astype(o_ref.dtype)


def _to_bf16(w, lead):
    *outer, R, C = w.shape
    free = outer[len(lead):]
    rb = R
    while rb * C * 4 > CAST_BLOCK_BYTES and rb % 2 == 0 and (rb // 2) % SUBLANES == 0:
        rb //= 2
    nfree = len(free)
    squeeze = (None,) * len(outer)
    in_map = lambda *g: (*lead, *g[:nfree], g[nfree], 0)
    out_map = lambda *g: (*g[:nfree], g[nfree], 0)
    return pl.pallas_call(
        _cast_kernel,
        grid=(*free, R // rb),
        in_specs=[pl.BlockSpec((*squeeze, rb, C), in_map)],
        out_specs=pl.BlockSpec(((None,) * nfree) + (rb, C), out_map),
        out_shape=jax.ShapeDtypeStruct((*free, R, C), BF16),
        compiler_params=_cparams(*(("arbitrary",) * (nfree + 1))),
        name="cast_bf16",
    )(w)


def _inproj_kernel(x_ref, w_ref, b_ref, qkv_ref, z_ref, xb_ref, acc_scr, *, nbatch):
    j = pl.program_id(1)

    @pl.when(j == 0)
    def _():
        xb_ref[...] = x_ref[...].astype(BF16)

    acc = jnp.dot(xb_ref[...], w_ref[...], preferred_element_type=F32) + b_ref[...]

    @pl.when(j < N_QKV_BLOCKS)
    def _():
        nslab, rows, _ = acc_scr.shape
        steps = rows // nbatch
        for c in range(nslab):
            acc_scr[c] = acc[:, c * LANES:(c + 1) * LANES]
        for b in range(nbatch):
            for c in range(nslab):
                qkv_ref[b, :, c * LANES:(c + 1) * LANES] = (
                    acc_scr[c, pl.ds(b, steps, stride=nbatch), :].astype(qkv_ref.dtype))

    @pl.when(j >= N_QKV_BLOCKS)
    def _():
        z_ref[...] = acc.astype(z_ref.dtype)


def _qkv_first(j):
    return jnp.where(j < N_QKV_BLOCKS, j + 1, jnp.where(j == N_QKV_BLOCKS, 0, j))


def _inproj(x, w, b, nbatch):
    T, K = x.shape
    N = w.shape[1]
    S = T // nbatch
    tm, tn = 2048, 512
    steps = tm // nbatch
    kern = functools.partial(_inproj_kernel, nbatch=nbatch)
    return pl.pallas_call(
        kern,
        grid=(T // tm, N // tn),
        in_specs=[
            pl.BlockSpec((tm, K), lambda i, j: (i, 0)),
            pl.BlockSpec((K, tn), lambda i, j: (0, _qkv_first(j))),
            pl.BlockSpec((1, tn), lambda i, j: (0, _qkv_first(j))),
        ],
        out_specs=[
            pl.BlockSpec((nbatch, steps, tn), lambda i, j: (0, i, jnp.minimum(j, N_QKV_BLOCKS - 1))),
            pl.BlockSpec((tm, tn), lambda i, j: (i, jnp.maximum(j - N_QKV_BLOCKS, 0))),
        ],
        out_shape=[jax.ShapeDtypeStruct((nbatch, S, QKV_WIDTH), BF16),
                   jax.ShapeDtypeStruct((T, Z_WIDTH), BF16)],
        scratch_shapes=[pltpu.VMEM((tm, K), BF16), pltpu.VMEM((tn // LANES, tm, LANES), F32)],
        compiler_params=_cparams("arbitrary", "arbitrary"),
        name="inproj",
    )(x, w, b)


def _attn_block(i, q_ref, k_ref, v_ref, o_ref, kmh_ref, kml_ref, kaug_ref, s_scr, p_scr,
                *, nb, blk, dh, n_heads, topk):
    pair = pl.program_id(1)
    W = 2 * dh
    lane = lax.broadcasted_iota(jnp.int32, (1, W), 1)
    head_of_lane = lane // dh
    nt = (((1,), (1,)), ((), ()))

    if i == 0:
        kmh_ref[...] = jnp.zeros_like(kmh_ref)
        kml_ref[...] = jnp.zeros_like(kml_ref)
        for n in range(nb):
            kb = k_ref[n * blk:(n + 1) * blk, :].astype(F32)
            mean = jnp.sum(kb, axis=0, keepdims=True) * (1.0 / blk)
            for hl in range(2):
                row = jnp.where(head_of_lane == hl, mean, 0.0)
                hi = row.astype(BF16)
                lo = (row - hi.astype(F32)).astype(BF16)
                kmh_ref[hl * nb + n:hl * nb + n + 1, :] = hi
                kml_ref[hl * nb + n:hl * nb + n + 1, :] = lo

    par = i % 2
    q = q_ref[i * blk:(i + 1) * blk, :] * jnp.asarray(dh ** -0.5, BF16)
    slopes = []
    for hl in range(2):
        h_f = (pair * 2 + hl + 1).astype(F32)
        slopes.append(jnp.exp2(jnp.full((1, blk), -8.0 / n_heads, F32) * h_f))

    ones = jnp.where((lane == 0) | (lane == dh), 1.0, 0.0)
    if i > 0:
        gate_t = (lax.dot_general(kmh_ref[...], q, nt, preferred_element_type=F32)
                  + lax.dot_general(kml_ref[...], q, nt, preferred_element_type=F32))
        row = lax.broadcasted_iota(jnp.int32, (nb, blk), 0)
        rowdist = ((row - i) * blk).astype(F32)
        parts = []
        for hl in range(2):
            g = gate_t[hl * nb:(hl + 1) * nb, :]
            cnt = jnp.zeros((nb, blk), F32)
            for n2 in range(i):
                r = g[n2:n2 + 1, :]
                cnt = cnt + jnp.where(r > g, 1.0,
                                      jnp.where(r == g, jnp.where(n2 < row, 1.0, 0.0), 0.0))
            sel = jnp.where(cnt < float(topk), 1.0, 0.0)
            parts.append(jnp.where(row < i, jnp.where(sel > 0.5, slopes[hl] * rowdist, NEG), 0.0))
        zrows = lambda n: jnp.zeros((n, blk), F32)
        bias_t = jnp.concatenate([zrows(nb), parts[1], zrows(dh - 2 * nb), zrows(nb), parts[0],
                                  zrows(dh - 2 * nb)], axis=0)
        aug = (bias_t.T + ones).astype(BF16)
    else:
        aug = jnp.broadcast_to(ones, (blk, W)).astype(BF16)

    r_io = lax.broadcasted_iota(jnp.int32, (blk, blk), 0)
    c_io = lax.broadcasted_iota(jnp.int32, (blk, blk), 1)
    nkeys = (i + 1) * blk

    outs = []
    for hl in range(2):
        qh = jnp.where(head_of_lane == hl, q, aug)
        macc = None
        for n in range(i + 1):
            s = lax.dot_general(qh, kaug_ref[hl, n * blk:(n + 1) * blk, :], nt,
                                preferred_element_type=F32)
            if n == i:
                s = jnp.where(c_io <= r_io, s, NEG)
            s_scr[par, hl, :, n * blk:(n + 1) * blk] = s
            t = jnp.maximum(s[:, :LANES], s[:, LANES:])
            macc = t if macc is None else jnp.maximum(macc, t)
        m = jnp.max(macc, axis=1, keepdims=True)
        lacc = None
        for n in range(i + 1):
            p = jnp.exp(s_scr[par, hl, :, n * blk:(n + 1) * blk] - m)
            p_scr[par, hl, :, n * blk:(n + 1) * blk] = p.astype(BF16)
            t = p[:, :LANES] + p[:, LANES:]
            lacc = t if lacc is None else lacc + t
        l = jnp.sum(lacc, axis=1, keepdims=True)
        acc = jnp.dot(p_scr[par, hl, :, 0:nkeys], v_ref[0:nkeys, :], preferred_element_type=F32)
        outs.append(acc * (1.0 / l))

    o_ref[i * blk:(i + 1) * blk, :] = (
        jnp.where(head_of_lane == 0, outs[0], outs[1]).astype(o_ref.dtype))


def _attn_aug_keys(k_ref, kaug_ref, *, nb, blk, dh, n_heads):
    pair = pl.program_id(1)
    lane = lax.broadcasted_iota(jnp.int32, (1, 2 * dh), 1)
    key_pos = lax.broadcasted_iota(jnp.int32, (blk, 1), 0).astype(F32)
    for hl in range(2):
        h_f = (pair * 2 + hl + 1).astype(F32)
        slope = jnp.exp2(jnp.full((blk, 1), -8.0 / n_heads, F32) * h_f)
        base = (1 - hl) * dh
        for n in range(nb):
            extra = jnp.where(lane == base, slope * key_pos,
                              jnp.where(lane == base + nb + n, 1.0, 0.0))
            kaug_ref[hl, n * blk:(n + 1) * blk, :] = jnp.where(
                lane // dh == hl, k_ref[n * blk:(n + 1) * blk, :], extra.astype(BF16))


def _attn_kernel(q_ref, k_ref, v_ref, o_ref, kmh_ref, kml_ref, kaug_ref, s_scr, p_scr,
                 *, nb, blk, dh, n_heads, topk):
    _attn_aug_keys(k_ref, kaug_ref, nb=nb, blk=blk, dh=dh, n_heads=n_heads)
    for i in range(nb):
        _attn_block(i, q_ref, k_ref, v_ref, o_ref, kmh_ref, kml_ref, kaug_ref, s_scr, p_scr,
                    nb=nb, blk=blk, dh=dh, n_heads=n_heads, topk=topk)


def _attention(qkv):
    B, S, _ = qkv.shape
    blk, dh = MOBA_BLOCK, HEAD_DIM
    nb = S // blk
    assert nb * blk == S and nb == SUBLANES and 2 * dh == LANES
    assert 8 % ATT_HEADS == 0 and 2 * nb + 1 <= dh
    n_pairs = ATT_HEADS // 2
    kern = functools.partial(_attn_kernel, nb=nb, blk=blk, dh=dh, n_heads=ATT_HEADS, topk=MOBA_TOPK)
    return pl.pallas_call(
        kern,
        grid=(B, n_pairs),
        in_specs=[
            pl.BlockSpec((None, S, LANES), lambda b, p: (b, 0, p)),
            pl.BlockSpec((None, S, LANES), lambda b, p: (b, 0, n_pairs + p)),
            pl.BlockSpec((None, S, LANES), lambda b, p: (b, 0, 2 * n_pairs + p)),
        ],
        out_specs=pl.BlockSpec((None, S, LANES), lambda b, p: (b, 0, p)),
        out_shape=jax.ShapeDtypeStruct((B, S, ATT_WIDTH), F32),
        scratch_shapes=[
            pltpu.VMEM((LANES, LANES), BF16),
            pltpu.VMEM((LANES, LANES), BF16),
            pltpu.VMEM((2, S, LANES), BF16),
            pltpu.VMEM((2, 2, blk, nb * blk), F32),
            pltpu.VMEM((2, 2, blk, nb * blk), BF16),
        ],
        compiler_params=_cparams("arbitrary", "arbitrary"),
        name="moba_attn",
    )(qkv, qkv, qkv)


POOL_HALO_STEPS = max(POOL_WINDOWS)


def _pool_kernel(u_ref, w_ref, scale_ref, o_ref, ext_scr, *, steps, nbatch):
    c = pl.program_id(0)
    R = steps * nbatch
    H = POOL_HALO_STEPS * nbatch

    @pl.when(c == 0)
    def _():
        ext_scr[0:H, :] = jnp.zeros((H, POOL_WIDTH), F32)

    @pl.when(c > 0)
    def _():
        ext_scr[0:H, :] = ext_scr[R:R + H, :]

    ext_scr[H:H + R, :] = u_ref[...].astype(F32)

    t_glob = c * steps + lax.broadcasted_iota(jnp.int32, (R, 1), 0) // nbatch
    outs = []
    for gi, w in enumerate(POOL_WINDOWS):
        cols = slice(gi * POOL_GROUP, (gi + 1) * POOL_GROUP)
        u = ext_scr[H:H + R, cols]
        acc = u
        for k in range(1, w):
            acc = acc + ext_scr[H - k * nbatch:H - k * nbatch + R, cols]
        count = jnp.minimum(t_glob + 1, w).astype(F32)
        d = acc / count - u
        outs.append(jnp.dot(d.astype(BF16), w_ref[gi], preferred_element_type=F32))
    y = jnp.concatenate(outs, axis=1) * scale_ref[...]
    o_ref[...] = y.astype(o_ref.dtype)


def _pool(z, w_pool, pool_scale, nbatch):
    T = z.shape[0]
    steps = 64
    R = steps * nbatch
    H = POOL_HALO_STEPS * nbatch
    kern = functools.partial(_pool_kernel, steps=steps, nbatch=nbatch)
    return pl.pallas_call(
        kern,
        grid=(T // R,),
        in_specs=[
            pl.BlockSpec((R, 512), lambda c: (c, COL_POOL)),
            pl.BlockSpec((len(POOL_WINDOWS), POOL_GROUP, POOL_GROUP), lambda c: (0, 0, 0)),
            pl.BlockSpec((1, POOL_WIDTH), lambda c: (0, 0)),
        ],
        out_specs=pl.BlockSpec((R, POOL_WIDTH), lambda c: (c, 0)),
        out_shape=jax.ShapeDtypeStruct((T, POOL_WIDTH), BF16),
        scratch_shapes=[pltpu.VMEM((H + R, POOL_WIDTH), F32)],
        compiler_params=_cparams("arbitrary"),
        name="pool_mixer",
    )(z, w_pool, pool_scale)


def _s5_prep_kernel(ar_ref, ai_ref, ldt_ref, br_ref, bi_ref, lr_ref, li_ref, bbr_ref, bbi_ref):
    ar, ai = ar_ref[...], ai_ref[...]
    dt = jnp.exp(ldt_ref[...])
    mag = jnp.exp(ar * dt)
    lr = mag * jnp.cos(ai * dt)
    li = mag * jnp.sin(ai * dt)
    den = ar * ar + ai * ai
    nr = lr - 1.0
    fr = (nr * ar + li * ai) / den
    fi = (li * ar - nr * ai) / den
    br, bi = br_ref[...], bi_ref[...]
    lr_ref[...] = lr
    li_ref[...] = li
    bbr_ref[...] = fr * br - fi * bi
    bbi_ref[...] = fr * bi + fi * br


def _s5_prep(a_re, a_im, log_dt, b_re, b_im):
    L, G, N = a_re.shape
    P = b_re.shape[-1]
    rows, cols = L * G, N * P
    exp_n = lambda a: jnp.broadcast_to(a[..., None], (L, G, N, P)).reshape(rows, cols)
    ldt = jnp.broadcast_to(log_dt[..., None, None], (L, G, N, P)).reshape(rows, cols)
    shp = jax.ShapeDtypeStruct((rows, cols), F32)
    spec = pl.BlockSpec((rows, cols), lambda: (0, 0))
    lr, li, bbr, bbi = pl.pallas_call(
        _s5_prep_kernel,
        in_specs=[spec] * 5,
        out_specs=[spec] * 4,
        out_shape=[shp] * 4,
        name="s5_discretise",
    )(exp_n(a_re), exp_n(a_im), ldt, b_re.reshape(rows, cols), b_im.reshape(rows, cols))
    lr = lr.reshape(L, G, N, P)[..., 0].reshape(L, G * N)
    li = li.reshape(L, G, N, P)[..., 0].reshape(L, G * N)
    return lr, li, bbr.reshape(L, G, N, P), bbi.reshape(L, G, N, P)


S5_GPB = LANES // SSM_GROUP
S5_NBLK = SSM_WIDTH // LANES
S5_SB = S5_GPB * SSM_STATE
S5_NS = SSM_GROUPS * SSM_STATE


def _s5_layout(lr, li, bbr, bbi, c_re, c_im, nbatch):
    eye = jnp.eye(S5_GPB, dtype=F32)

    def bmat(bb):
        bb = bb.reshape(S5_NBLK, S5_GPB, SSM_STATE, SSM_GROUP)
        m = jnp.einsum('jgnp,gh->jgphn', bb, eye)
        return m.reshape(S5_NBLK, LANES, S5_SB)

    def cmat(c):
        c = c.reshape(S5_NBLK, S5_GPB, SSM_GROUP, SSM_STATE)
        m = jnp.einsum('jgpn,gh->jgnhp', c, eye)
        return m.reshape(S5_NBLK, S5_SB, LANES)

    b_mat = jnp.concatenate([bmat(bbr), bmat(bbi)], axis=2).astype(BF16)
    c_mat = jnp.concatenate([cmat(c_re), -cmat(c_im)], axis=1).astype(BF16)
    lam = jnp.concatenate([jnp.broadcast_to(lr[None, :], (nbatch, S5_NS)),
                           jnp.broadcast_to(li[None, :], (nbatch, S5_NS))], axis=0)
    return b_mat, c_mat, lam


S5_SCAN_COLS = 1024


def _s5_kernel(u_ref, bmat_ref, lam_ref, cmat_ref, d_ref, wglu_ref, bglu_ref, o_ref,
               x_scr, h_scr, *, steps, nbatch):
    @pl.when(pl.program_id(0) == 0)
    def _():
        h_scr[...] = jnp.zeros_like(h_scr)

    u = u_ref[...]
    for j in range(S5_NBLK):
        xj = jnp.dot(u[:, j * LANES:(j + 1) * LANES], bmat_ref[j], preferred_element_type=F32)
        x_scr[:, j * S5_SB:(j + 1) * S5_SB] = xj[:, :S5_SB]
        x_scr[:, S5_NS + j * S5_SB:S5_NS + (j + 1) * S5_SB] = xj[:, S5_SB:]

    for c in range(S5_NS // S5_SCAN_COLS):
        re = slice(c * S5_SCAN_COLS, (c + 1) * S5_SCAN_COLS)
        im = slice(S5_NS + c * S5_SCAN_COLS, S5_NS + (c + 1) * S5_SCAN_COLS)
        lr = lam_ref[0:nbatch, re]
        li = lam_ref[nbatch:2 * nbatch, re]

        def body(t, carry, re=re, im=im, lr=lr, li=li):
            hr, hi = carry
            r0 = pl.multiple_of(t * nbatch, nbatch)
            nhr = lr * hr - li * hi + x_scr[pl.ds(r0, nbatch), re]
            nhi = lr * hi + li * hr + x_scr[pl.ds(r0, nbatch), im]
            x_scr[pl.ds(r0, nbatch), re] = nhr
            x_scr[pl.ds(r0, nbatch), im] = nhi
            return nhr, nhi

        hr, hi = lax.fori_loop(0, steps, body, (h_scr[:, re], h_scr[:, im]), unroll=4)
        h_scr[:, re] = hr
        h_scr[:, im] = hi

    ys = []
    for j in range(S5_NBLK):
        hr = x_scr[:, j * S5_SB:(j + 1) * S5_SB].astype(BF16)
        hi = x_scr[:, S5_NS + j * S5_SB:S5_NS + (j + 1) * S5_SB].astype(BF16)
        ys.append(jnp.dot(hr, cmat_ref[j, 0:S5_SB, :], preferred_element_type=F32)
                  + jnp.dot(hi, cmat_ref[j, S5_SB:2 * S5_SB, :], preferred_element_type=F32))
    y = jnp.concatenate(ys, axis=1) + d_ref[...] * u.astype(F32)
    y = jax.nn.gelu(y, approximate=True)
    z = jnp.dot(y.astype(BF16), wglu_ref[...], preferred_element_type=F32) + bglu_ref[...]
    o_ref[...] = (z[:, :SSM_WIDTH] * jax.nn.sigmoid(z[:, SSM_WIDTH:])).astype(o_ref.dtype)


def _s5(z, b_mat, lam, c_mat, d_skip, w_glu, b_glu, nbatch):
    T = z.shape[0]
    steps = 32
    R = steps * nbatch
    kern = functools.partial(_s5_kernel, steps=steps, nbatch=nbatch)
    const = lambda *shape: pl.BlockSpec(shape, lambda c: (0,) * len(shape))
    return pl.pallas_call(
        kern,
        grid=(T // R,),
        in_specs=[
            pl.BlockSpec((R, 512), lambda c: (c, COL_SSM)),
            const(S5_NBLK, LANES, 2 * S5_SB),
            const(2 * nbatch, S5_NS),
            const(S5_NBLK, 2 * S5_SB, LANES),
            const(1, SSM_WIDTH),
            const(SSM_WIDTH, 2 * SSM_WIDTH),
            const(1, 2 * SSM_WIDTH),
        ],
        out_specs=pl.BlockSpec((R, SSM_WIDTH), lambda c: (c, 0)),
        out_shape=jax.ShapeDtypeStruct((T, SSM_WIDTH), BF16),
        scratch_shapes=[pltpu.VMEM((R, 2 * S5_NS), F32), pltpu.VMEM((nbatch, 2 * S5_NS), F32)],
        compiler_params=_cparams("arbitrary"),
        name="s5_mixer",
    )(z, b_mat, lam, c_mat, d_skip, w_glu, b_glu)


def _merge_kernel(h_ref, ya_ref, yb_ref, yc_ref, g0a, g0b, g1a, g1b, g2a, g2b,
                  wbr_ref, wout_ref, lng_ref, lnb_ref, o_ref, yb_scr):
    nbatch, steps, _ = yb_ref.shape
    nslab = yb_scr.shape[0]
    for b in range(nbatch):
        for c in range(nslab):
            yb_scr[c, pl.ds(b, steps, stride=nbatch), :] = yb_ref[b, :, c * LANES:(c + 1) * LANES]
    yb = jnp.concatenate([yb_scr[c] for c in range(nslab)], axis=1).astype(BF16)
    ys = (ya_ref[...], yb, yc_ref[...])
    gs = ((g0a, g0b), (g1a, g1b), (g2a, g2b))
    half = D_MODEL // 2
    merged = []
    for hcol in range(2):
        acc = None
        for n in range(N_BRANCH):
            proj = jnp.dot(ys[n], wbr_ref[n, :, hcol * half:(hcol + 1) * half],
                           preferred_element_type=F32)
            term = jax.nn.sigmoid(gs[n][hcol][...].astype(F32)) * proj
            acc = term if acc is None else acc + term
        merged.append(acc.astype(BF16))
    merged = jnp.concatenate(merged, axis=1)
    mix = jnp.dot(merged, wout_ref[...], preferred_element_type=F32)
    r = DN_ALPHA * h_ref[...] + mix
    o_ref[...] = _layer_norm(r, lng_ref[...], lnb_ref[...])


def _merge(h, y_a, y_b, y_c, z, w_branch, w_out, ln_g, ln_b):
    T = h.shape[0]
    nbatch = y_b.shape[0]
    tm = 512
    steps = tm // nbatch
    row = lambda w: pl.BlockSpec((tm, w), lambda i: (i, 0))
    gate = lambda k: pl.BlockSpec((tm, 512), lambda i, k=k: (i, COL_GATE + k))
    return pl.pallas_call(
        _merge_kernel,
        grid=(T // tm,),
        in_specs=[row(D_MODEL), row(512),
                  pl.BlockSpec((nbatch, steps, ATT_WIDTH), lambda i: (0, i, 0)),
                  row(512)]
        + [gate(k) for k in range(2 * N_BRANCH)]
        + [
            pl.BlockSpec((N_BRANCH, BRANCH_WIDTH, D_MODEL), lambda i: (0, 0, 0)),
            pl.BlockSpec((D_MODEL, D_MODEL), lambda i: (0, 0)),
            pl.BlockSpec((1, D_MODEL), lambda i: (0, 0)),
            pl.BlockSpec((1, D_MODEL), lambda i: (0, 0)),
        ],
        out_specs=row(D_MODEL),
        out_shape=jax.ShapeDtypeStruct((T, D_MODEL), F32),
        scratch_shapes=[pltpu.VMEM((ATT_WIDTH // LANES, tm, LANES), F32)],
        compiler_params=_cparams("arbitrary"),
        name="merge_ln",
    )(h, y_a, y_b, y_c, z, z, z, z, z, z, w_branch, w_out, ln_g, ln_b)


def _ffn_kernel(x_ref, w1_ref, w3_ref, w2_ref, lng_ref, lnb_ref, o_ref, xb_ref, acc_ref):
    f = pl.program_id(1)

    @pl.when(f == 0)
    def _():
        xb_ref[...] = x_ref[...].astype(BF16)
        acc_ref[...] = jnp.zeros_like(acc_ref)

    xb = xb_ref[...]
    a = jax.nn.silu(jnp.dot(xb, w1_ref[...], preferred_element_type=F32))
    a = a * jnp.dot(xb, w3_ref[...], preferred_element_type=F32)
    acc_ref[...] += jnp.dot(a.astype(BF16), w2_ref[...], preferred_element_type=F32)

    @pl.when(f == pl.num_programs(1) - 1)
    def _():
        r = DN_ALPHA * x_ref[...] + acc_ref[...]
        o_ref[...] = _layer_norm(r, lng_ref[...], lnb_ref[...])


FFN_TF = 1408


def _ffn(x, w1, w3, w2, ln_g, ln_b):
    T = x.shape[0]
    tm, tf = 512, FFN_TF
    return pl.pallas_call(
        _ffn_kernel,
        grid=(T // tm, D_FF // tf),
        in_specs=[
            pl.BlockSpec((tm, D_MODEL), lambda i, f: (i, 0)),
            pl.BlockSpec((D_MODEL, tf), lambda i, f: (0, f)),
            pl.BlockSpec((D_MODEL, tf), lambda i, f: (0, f)),
            pl.BlockSpec((tf, D_MODEL), lambda i, f: (f, 0)),
            pl.BlockSpec((1, D_MODEL), lambda i, f: (0, 0)),
            pl.BlockSpec((1, D_MODEL), lambda i, f: (0, 0)),
        ],
        out_specs=pl.BlockSpec((tm, D_MODEL), lambda i, f: (i, 0)),
        out_shape=jax.ShapeDtypeStruct((T, D_MODEL), F32),
        scratch_shapes=[pltpu.VMEM((tm, D_MODEL), BF16), pltpu.VMEM((tm, D_MODEL), F32)],
        compiler_params=_cparams("arbitrary", "arbitrary"),
        name="ffn_ln",
    )(x, w1, w3, w2, ln_g, ln_b)


def _router_kernel(x_ref, w_ref, b_ref, idx_ref, wgt_ref):
    logits = jnp.dot(x_ref[...], w_ref[...], preferred_element_type=F32,
                     precision=lax.Precision.HIGHEST) + b_ref[...]
    lane = lax.broadcasted_iota(jnp.int32, logits.shape, 1)
    l1 = jnp.where(lane < N_EXPERTS, logits, -jnp.inf)
    m1 = jnp.max(l1, axis=1, keepdims=True)
    i1 = jnp.min(jnp.where(l1 == m1, lane, LANES), axis=1, keepdims=True)
    l2 = jnp.where(lane == i1, -jnp.inf, l1)
    m2 = jnp.max(l2, axis=1, keepdims=True)
    i2 = jnp.min(jnp.where(l2 == m2, lane, LANES), axis=1, keepdims=True)
    e = jnp.exp(m2 - m1)
    w_top = 1.0 / (1.0 + e)
    w_sec = e / (1.0 + e)
    idx_ref[...] = jnp.where(lane == 0, i1, jnp.where(lane == 1, i2, 0))
    wgt_ref[...] = jnp.where(lane == 0, w_top, jnp.where(lane == 1, w_sec, 0.0))


def _router(x, w, b):
    T = x.shape[0]
    tm = 1024
    out = pl.BlockSpec((tm, LANES), lambda i: (i, 0))
    return pl.pallas_call(
        _router_kernel,
        grid=(T // tm,),
        in_specs=[
            pl.BlockSpec((tm, D_MODEL), lambda i: (i, 0)),
            pl.BlockSpec((D_MODEL, LANES), lambda i: (0, 0)),
            pl.BlockSpec((1, LANES), lambda i: (0, 0)),
        ],
        out_specs=[out, out],
        out_shape=[jax.ShapeDtypeStruct((T, LANES), jnp.int32),
                   jax.ShapeDtypeStruct((T, LANES), F32)],
        compiler_params=_cparams("arbitrary"),
        name="moe_router",
    )(x, w, b)


MOE_TM = 512
MOE_DMA_UNROLL = 8


def _route_plan(idx, wgt, T):
    n_assign = TOP_K * T
    n_tiles = n_assign // MOE_TM + N_EXPERTS
    e_flat = idx.T.reshape(-1)
    w_flat = wgt.T.reshape(-1)
    order = jnp.argsort(e_flat, stable=True).astype(jnp.int32)
    counts = jnp.sum((e_flat[:, None] == jnp.arange(N_EXPERTS)[None, :]).astype(jnp.int32), axis=0)
    group_start = jnp.cumsum(counts) - counts
    tiles_per = (counts + MOE_TM - 1) // MOE_TM
    tile_end = jnp.cumsum(tiles_per)
    tile_start = tile_end - tiles_per
    tile_id = jnp.arange(n_tiles, dtype=jnp.int32)
    used = tile_id < tile_end[-1]
    tile_expert = jnp.minimum(jnp.sum((tile_id[:, None] >= tile_end[None, :]).astype(jnp.int32), axis=1),
                              N_EXPERTS - 1)
    tile_first = (tile_id - tile_start[tile_expert]) * MOE_TM
    tile_rows = jnp.where(used, jnp.clip(counts[tile_expert] - tile_first, 0, MOE_TM), 0)
    last_expert = tile_expert[jnp.maximum(tile_end[-1] - 1, 0)]
    tile_expert = jnp.where(used, tile_expert, last_expert).astype(jnp.int32)
    j = jnp.arange(MOE_TM, dtype=jnp.int32)[None, :]
    valid = j < tile_rows[:, None]
    pos = jnp.clip(group_start[tile_expert][:, None] + tile_first[:, None] + j, 0, n_assign - 1)
    a = jnp.where(valid, order[pos], 0)
    src_tok = jnp.where(a >= T, a - T, a).astype(jnp.int32)
    row_w = jnp.where(valid, w_flat[a], 0.0)
    return (tile_expert, tile_rows.astype(jnp.int32), src_tok[:, None, :], a.astype(jnp.int32)[:, None, :],
            row_w[:, :, None])


def _moe_kernel(te_ref, tn_ref, src_ref, src_next_ref, dst_ref, roww_ref, x_hbm, w1_ref, w3_ref, w2_ref,
                buf_hbm, xg, xb, acc, y_scr, gsem, ssem):
    t = pl.program_id(0)
    f = pl.program_id(1)
    n_tiles = pl.num_programs(0)
    last_f = pl.num_programs(1) - 1
    nrows = tn_ref[t]
    slot = t % 2

    def start_gather(idx_ref, buf_slot, r):
        pltpu.make_async_copy(x_hbm.at[pl.ds(idx_ref[0, r], 1), :], xg.at[buf_slot, pl.ds(r, 1), :],
                              gsem.at[buf_slot]).start()

    def start_scatter(r):
        pltpu.make_async_copy(y_scr.at[pl.ds(r, 1), :], buf_hbm.at[pl.ds(dst_ref[0, r], 1), :], ssem).start()

    def for_rows(n, fn):
        ngroups = n // MOE_DMA_UNROLL

        def group(g, c):
            base = pl.multiple_of(g * MOE_DMA_UNROLL, MOE_DMA_UNROLL)
            for u in range(MOE_DMA_UNROLL):
                fn(base + u)
            return c

        def single(r, c):
            fn(r)
            return c

        lax.fori_loop(0, ngroups, group, 0)
        lax.fori_loop(ngroups * MOE_DMA_UNROLL, n, single, 0)

    def wait_rows(n, src, dst, sem):
        p = MOE_TM
        while p >= 1:
            @pl.when((n & p) != 0)
            def _(p=p):
                pltpu.make_async_copy(src.at[pl.ds(0, p), :], dst.at[pl.ds(0, p), :], sem).wait()
            p //= 2

    @pl.when((t == 0) & (f == 0))
    def _():
        xg[...] = jnp.zeros_like(xg)
        for_rows(nrows, functools.partial(start_gather, src_ref, 0))

    @pl.when(f == 0)
    def _():
        wait_rows(nrows, x_hbm, xg.at[slot], gsem.at[slot])

        @pl.when(nrows > 0)
        def _():
            xb[...] = xg[slot].astype(BF16)

        @pl.when(t + 1 < n_tiles)
        def _():
            for_rows(tn_ref[t + 1], functools.partial(start_gather, src_next_ref, 1 - slot))

    @pl.when(nrows > 0)
    def _():
        x = xb[...]
        a = jax.nn.silu(jnp.dot(x, w1_ref[...], preferred_element_type=F32))
        a = a * jnp.dot(x, w3_ref[...], preferred_element_type=F32)
        y = jnp.dot(a.astype(BF16), w2_ref[...], preferred_element_type=F32)

        @pl.when(f == 0)
        def _():
            acc[...] = y

        @pl.when(f > 0)
        def _():
            acc[...] += y

    @pl.when(f == last_f)
    def _():
        @pl.when(t > 0)
        def _():
            wait_rows(tn_ref[t - 1], y_scr, buf_hbm, ssem)

        @pl.when(nrows > 0)
        def _():
            y_scr[...] = acc[...] * roww_ref[...]
            for_rows(nrows, start_scatter)

        @pl.when(t == n_tiles - 1)
        def _():
            wait_rows(nrows, y_scr, buf_hbm, ssem)


def _moe_experts(x, plan, w1, w3, w2):
    T = x.shape[0]
    tile_expert, tile_rows, src_tok, dst_row, row_w = plan
    n_tiles = tile_expert.shape[0]
    tf = FFN_TF
    smem_rows = pl.BlockSpec((None, 1, MOE_TM), lambda t, f, te, tn: (t, 0, 0), memory_space=pltpu.SMEM)
    smem_next = pl.BlockSpec((None, 1, MOE_TM), lambda t, f, te, tn: (jnp.minimum(t + 1, n_tiles - 1), 0, 0),
                             memory_space=pltpu.SMEM)
    grid_spec = pltpu.PrefetchScalarGridSpec(
        num_scalar_prefetch=2,
        grid=(n_tiles, D_FF // tf),
        in_specs=[
            smem_rows,
            smem_next,
            smem_rows,
            pl.BlockSpec((None, MOE_TM, 1), lambda t, f, te, tn: (t, 0, 0)),
            pl.BlockSpec(memory_space=pl.ANY),
            pl.BlockSpec((None, D_MODEL, tf), lambda t, f, te, tn: (te[t], 0, f)),
            pl.BlockSpec((None, D_MODEL, tf), lambda t, f, te, tn: (te[t], 0, f)),
            pl.BlockSpec((None, tf, D_MODEL), lambda t, f, te, tn: (te[t], f, 0)),
        ],
        out_specs=pl.BlockSpec(memory_space=pl.ANY),
        scratch_shapes=[
            pltpu.VMEM((2, MOE_TM, D_MODEL), F32),
            pltpu.VMEM((MOE_TM, D_MODEL), BF16),
            pltpu.VMEM((MOE_TM, D_MODEL), F32),
            pltpu.VMEM((MOE_TM, D_MODEL), F32),
            pltpu.SemaphoreType.DMA((2,)),
            pltpu.SemaphoreType.DMA(()),
        ],
    )
    return pl.pallas_call(
        _moe_kernel,
        grid_spec=grid_spec,
        out_shape=jax.ShapeDtypeStruct((TOP_K * T, D_MODEL), F32),
        compiler_params=_cparams("arbitrary", "arbitrary"),
        name="moe_experts",
    )(tile_expert, tile_rows, src_tok, src_tok, dst_row, row_w, x, w1, w3, w2)


def _moe_combine_kernel(x_ref, y0_ref, y1_ref, lng_ref, lnb_ref, o_ref):
    r = DN_ALPHA * x_ref[...] + (y0_ref[...] + y1_ref[...])
    o_ref[...] = _layer_norm(r, lng_ref[...], lnb_ref[...])


def _moe_combine(x, buf, ln_g, ln_b):
    T = x.shape[0]
    tm = 1024
    nblk = T // tm
    return pl.pallas_call(
        _moe_combine_kernel,
        grid=(nblk,),
        in_specs=[
            pl.BlockSpec((tm, D_MODEL), lambda i: (i, 0)),
            pl.BlockSpec((tm, D_MODEL), lambda i: (i, 0)),
            pl.BlockSpec((tm, D_MODEL), lambda i: (i + nblk, 0)),
            pl.BlockSpec((1, D_MODEL), lambda i: (0, 0)),
            pl.BlockSpec((1, D_MODEL), lambda i: (0, 0)),
        ],
        out_specs=pl.BlockSpec((tm, D_MODEL), lambda i: (i, 0)),
        out_shape=jax.ShapeDtypeStruct((T, D_MODEL), F32),
        compiler_params=_cparams("arbitrary"),
        name="moe_combine_ln",
    )(x, buf, buf, ln_g, ln_b)


def kernel(x, ln1_g, ln1_b, w_in, b_in, w_pool, pool_scale, ssm_a_re, ssm_a_im, ssm_log_dt,
           ssm_b_re, ssm_b_im, ssm_c_re, ssm_c_im, ssm_d, w_glu, b_glu, w_branch, w_out,
           ln2_g, ln2_b, ffn_w1, ffn_w3, ffn_w2, moe_router, moe_router_b, moe_w1, moe_w3, moe_w2):
    B, S, D = x.shape
    T = B * S
    assert B == SUBLANES and D == D_MODEL
    h = x.transpose(1, 0, 2).reshape(T, D)

    lr, li, bbr, bbi = _s5_prep(ssm_a_re, ssm_a_im, ssm_log_dt, ssm_b_re, ssm_b_im)
    row = lambda a: a.reshape(1, -1)

    for i in range(DEPTH):
        qkv, z = _inproj(h, _to_bf16(w_in, (i,)), row(b_in[i]), B)
        y_b = _attention(qkv)
        y_a = _pool(z, w_pool[i].astype(BF16), row(pool_scale[i]), B)
        b_mat, c_mat, lam = _s5_layout(lr[i], li[i], bbr[i], bbi[i], ssm_c_re[i], ssm_c_im[i], B)
        y_c = _s5(z, b_mat, lam, c_mat, row(ssm_d[i]), w_glu[i].astype(BF16), row(b_glu[i]), B)
        h = _merge(h, y_a, y_b, y_c, z, w_branch[i].astype(BF16), w_out[i].astype(BF16),
                   row(ln1_g[i]), row(ln1_b[i]))
        j = i // 2
        if i % 2 == 0:
            h = _ffn(h, _to_bf16(ffn_w1, (j,)), _to_bf16(ffn_w3, (j,)), _to_bf16(ffn_w2, (j,)),
                     row(ln2_g[i]), row(ln2_b[i]))
        else:
            rw = jnp.pad(moe_router[j], ((0, 0), (0, LANES - N_EXPERTS)))
            rb = jnp.pad(moe_router_b[j], (0, LANES - N_EXPERTS)).reshape(1, LANES)
            idx, wgt = _router(h, rw, rb)
            plan = _route_plan(idx[:, :TOP_K], wgt[:, :TOP_K], T)
            buf = _moe_experts(h, plan, _to_bf16(moe_w1, (j,)), _to_bf16(moe_w3, (j,)),
                               _to_bf16(moe_w2, (j,)))
            h = _moe_combine(h, buf, row(ln2_g[i]), row(ln2_b[i]))
    return h.reshape(S, B, D).transpose(1, 0, 2)
```

```python
import functools

import jax
import jax.numpy as jnp
from jax import lax
from jax.experimental import pallas as pl
from jax.experimental.pallas import tpu as pltpu

F32 = jnp.float32
BF16 = jnp.bfloat16

D_MODEL = 1024
DEPTH = 4
POOL_WINDOWS = (2, 4, 8, 16)
POOL_WIDTH = 512
POOL_GROUP = POOL_WIDTH // len(POOL_WINDOWS)
ATT_HEADS = 8
HEAD_DIM = 64
ATT_WIDTH = ATT_HEADS * HEAD_DIM
MOBA_BLOCK = 256
MOBA_TOPK = 3
SSM_WIDTH = 512
SSM_GROUP = 16
SSM_GROUPS = SSM_WIDTH // SSM_GROUP
SSM_STATE = 64
N_BRANCH = 3
BRANCH_WIDTH = 512
IN_WIDTH = POOL_WIDTH + 3 * ATT_WIDTH + SSM_WIDTH + N_BRANCH * D_MODEL
D_FF = 2816
N_EXPERTS = 8
TOP_K = 2
DN_ALPHA = (2 * DEPTH) ** 0.25
LN_EPS = 1e-5

LANES = 128
SUBLANES = 8
VMEM_LIMIT = 48 * 1024 * 1024

NEG = -1e30

COL_POOL, COL_SSM, COL_GATE = 0, 1, 2
QKV_WIDTH = 3 * ATT_WIDTH
Z_WIDTH = IN_WIDTH - QKV_WIDTH
N_QKV_BLOCKS = QKV_WIDTH // 512


def _cparams(*sem):
    return pltpu.CompilerParams(dimension_semantics=sem, vmem_limit_bytes=VMEM_LIMIT)


def _layer_norm(r, g, b):
    mu = jnp.mean(r, axis=-1, keepdims=True)
    c = r - mu
    var = jnp.mean(c * c, axis=-1, keepdims=True)
    return c * lax.rsqrt(var + LN_EPS) * g + b


CAST_BLOCK_BYTES = 6 * 1024 * 1024


def _cast_kernel(w_ref, o_ref):
    o_ref[...] = w_ref[...].astype(o_ref.dtype)


def _to_bf16(w, lead):
    *outer, R, C = w.shape
    free = outer[len(lead):]
    rb = R
    while rb * C * 4 > CAST_BLOCK_BYTES and rb % 2 == 0 and (rb // 2) % SUBLANES == 0:
        rb //= 2
    nfree = len(free)
    squeeze = (None,) * len(outer)
    in_map = lambda *g: (*lead, *g[:nfree], g[nfree], 0)
    out_map = lambda *g: (*g[:nfree], g[nfree], 0)
    return pl.pallas_call(
        _cast_kernel,
        grid=(*free, R // rb),
        in_specs=[pl.BlockSpec((*squeeze, rb, C), in_map)],
        out_specs=pl.BlockSpec(((None,) * nfree) + (rb, C), out_map),
        out_shape=jax.ShapeDtypeStruct((*free, R, C), BF16),
        compiler_params=_cparams(*(("arbitrary",) * (nfree + 1))),
        name="cast_bf16",
    )(w)


def _inproj_kernel(x_ref, w_ref, b_ref, qkv_ref, z_ref, xb_ref, acc_scr, *, nbatch):
    j = pl.program_id(1)

    @pl.when(j == 0)
    def _():
        xb_ref[...] = x_ref[...].astype(BF16)

    acc = jnp.dot(xb_ref[...], w_ref[...], preferred_element_type=F32) + b_ref[...]

    @pl.when(j < N_QKV_BLOCKS)
    def _():
        nslab, rows, _ = acc_scr.shape
        steps = rows // nbatch
        for c in range(nslab):
            acc_scr[c] = acc[:, c * LANES:(c + 1) * LANES]
        for b in range(nbatch):
            for c in range(nslab):
                qkv_ref[b, :, c * LANES:(c + 1) * LANES] = (
                    acc_scr[c, pl.ds(b, steps, stride=nbatch), :].astype(qkv_ref.dtype))

    @pl.when(j >= N_QKV_BLOCKS)
    def _():
        z_ref[...] = acc.astype(z_ref.dtype)


def _qkv_first(j):
    return jnp.where(j < N_QKV_BLOCKS, j + 1, jnp.where(j == N_QKV_BLOCKS, 0, j))


def _inproj(x, w, b, nbatch):
    T, K = x.shape
    N = w.shape[1]
    S = T // nbatch
    tm, tn = 2048, 512
    steps = tm // nbatch
    kern = functools.partial(_inproj_kernel, nbatch=nbatch)
    return pl.pallas_call(
        kern,
        grid=(T // tm, N // tn),
        in_specs=[
            pl.BlockSpec((tm, K), lambda i, j: (i, 0)),
            pl.BlockSpec((K, tn), lambda i, j: (0, _qkv_first(j))),
            pl.BlockSpec((1, tn), lambda i, j: (0, _qkv_first(j))),
        ],
        out_specs=[
            pl.BlockSpec((nbatch, steps, tn), lambda i, j: (0, i, jnp.minimum(j, N_QKV_BLOCKS - 1))),
            pl.BlockSpec((tm, tn), lambda i, j: (i, jnp.maximum(j - N_QKV_BLOCKS, 0))),
        ],
        out_shape=[jax.ShapeDtypeStruct((nbatch, S, QKV_WIDTH), BF16),
                   jax.ShapeDtypeStruct((T, Z_WIDTH), BF16)],
        scratch_shapes=[pltpu.VMEM((tm, K), BF16), pltpu.VMEM((tn // LANES, tm, LANES), F32)],
        compiler_params=_cparams("arbitrary", "arbitrary"),
        name="inproj",
    )(x, w, b)


def _attn_block(i, q_ref, k_ref, v_ref, o_ref, kmh_ref, kml_ref, kaug_ref, s_scr, p_scr,
                *, nb, blk, dh, n_heads, topk):
    pair = pl.program_id(1)
    W = 2 * dh
    lane = lax.broadcasted_iota(jnp.int32, (1, W), 1)
    head_of_lane = lane // dh
    nt = (((1,), (1,)), ((), ()))

    if i == 0:
        kmh_ref[...] = jnp.zeros_like(kmh_ref)
        kml_ref[...] = jnp.zeros_like(kml_ref)
        for n in range(nb):
            kb = k_ref[n * blk:(n + 1) * blk, :].astype(F32)
            mean = jnp.sum(kb, axis=0, keepdims=True) * (1.0 / blk)
            for hl in range(2):
                row = jnp.where(head_of_lane == hl, mean, 0.0)
                hi = row.astype(BF16)
                lo = (row - hi.astype(F32)).astype(BF16)
                kmh_ref[hl * nb + n:hl * nb + n + 1, :] = hi
                kml_ref[hl * nb + n:hl * nb + n + 1, :] = lo

    par = i % 2
    q = q_ref[i * blk:(i + 1) * blk, :] * jnp.asarray(dh ** -0.5, BF16)
    slopes = []
    for hl in range(2):
        h_f = (pair * 2 + hl + 1).astype(F32)
        slopes.append(jnp.exp2(jnp.full((1, blk), -8.0 / n_heads, F32) * h_f))

    ones = jnp.where((lane == 0) | (lane == dh), 1.0, 0.0)
    if i > 0:
        gate_t = (lax.dot_general(kmh_ref[...], q, nt, preferred_element_type=F32)
                  + lax.dot_general(kml_ref[...], q, nt, preferred_element_type=F32))
        row = lax.broadcasted_iota(jnp.int32, (nb, blk), 0)
        rowdist = ((row - i) * blk).astype(F32)
        parts = []
        for hl in range(2):
            g = gate_t[hl * nb:(hl + 1) * nb, :]
            cnt = jnp.zeros((nb, blk), F32)
            for n2 in range(i):
                r = g[n2:n2 + 1, :]
                cnt = cnt + jnp.where(r > g, 1.0,
                                      jnp.where(r == g, jnp.where(n2 < row, 1.0, 0.0), 0.0))
            sel = jnp.where(cnt < float(topk), 1.0, 0.0)
            parts.append(jnp.where(row < i, jnp.where(sel > 0.5, slopes[hl] * rowdist, NEG), 0.0))
        zrows = lambda n: jnp.zeros((n, blk), F32)
        bias_t = jnp.concatenate([zrows(nb), parts[1], zrows(dh - 2 * nb), zrows(nb), parts[0],
                                  zrows(dh - 2 * nb)], axis=0)
        aug = (bias_t.T + ones).astype(BF16)
    else:
        aug = jnp.broadcast_to(ones, (blk, W)).astype(BF16)

    r_io = lax.broadcasted_iota(jnp.int32, (blk, blk), 0)
    c_io = lax.broadcasted_iota(jnp.int32, (blk, blk), 1)
    nkeys = (i + 1) * blk

    outs = []
    for hl in range(2):
        qh = jnp.where(head_of_lane == hl, q, aug)
        macc = None
        for n in range(i + 1):
            s = lax.dot_general(qh, kaug_ref[hl, n * blk:(n + 1) * blk, :], nt,
                                preferred_element_type=F32)
            if n == i:
                s = jnp.where(c_io <= r_io, s, NEG)
            s_scr[par, hl, :, n * blk:(n + 1) * blk] = s
            t = jnp.maximum(s[:, :LANES], s[:, LANES:])
            macc = t if macc is None else jnp.maximum(macc, t)
        m = jnp.max(macc, axis=1, keepdims=True)
        lacc = None
        for n in range(i + 1):
            p = jnp.exp(s_scr[par, hl, :, n * blk:(n + 1) * blk] - m)
            p_scr[par, hl, :, n * blk:(n + 1) * blk] = p.astype(BF16)
            t = p[:, :LANES] + p[:, LANES:]
            lacc = t if lacc is None else lacc + t
        l = jnp.sum(lacc, axis=1, keepdims=True)
        acc = jnp.dot(p_scr[par, hl, :, 0:nkeys], v_ref[0:nkeys, :], preferred_element_type=F32)
        outs.append(acc * (1.0 / l))

    o_ref[i * blk:(i + 1) * blk, :] = (
        jnp.where(head_of_lane == 0, outs[0], outs[1]).astype(o_ref.dtype))


def _attn_aug_keys(k_ref, kaug_ref, *, nb, blk, dh, n_heads):
    pair = pl.program_id(1)
    lane = lax.broadcasted_iota(jnp.int32, (1, 2 * dh), 1)
    key_pos = lax.broadcasted_iota(jnp.int32, (blk, 1), 0).astype(F32)
    for hl in range(2):
        h_f = (pair * 2 + hl + 1).astype(F32)
        slope = jnp.exp2(jnp.full((blk, 1), -8.0 / n_heads, F32) * h_f)
        base = (1 - hl) * dh
        for n in range(nb):
            extra = jnp.where(lane == base, slope * key_pos,
                              jnp.where(lane == base + nb + n, 1.0, 0.0))
            kaug_ref[hl, n * blk:(n + 1) * blk, :] = jnp.where(
                lane // dh == hl, k_ref[n * blk:(n + 1) * blk, :], extra.astype(BF16))


def _attn_kernel(q_ref, k_ref, v_ref, o_ref, kmh_ref, kml_ref, kaug_ref, s_scr, p_scr,
                 *, nb, blk, dh, n_heads, topk):
    _attn_aug_keys(k_ref, kaug_ref, nb=nb, blk=blk, dh=dh, n_heads=n_heads)
    for i in range(nb):
        _attn_block(i, q_ref, k_ref, v_ref, o_ref, kmh_ref, kml_ref, kaug_ref, s_scr, p_scr,
                    nb=nb, blk=blk, dh=dh, n_heads=n_heads, topk=topk)


def _attention(qkv):
    B, S, _ = qkv.shape
    blk, dh = MOBA_BLOCK, HEAD_DIM
    nb = S // blk
    assert nb * blk == S and nb == SUBLANES and 2 * dh == LANES
    assert 8 % ATT_HEADS == 0 and 2 * nb + 1 <= dh
    n_pairs = ATT_HEADS // 2
    kern = functools.partial(_attn_kernel, nb=nb, blk=blk, dh=dh, n_heads=ATT_HEADS, topk=MOBA_TOPK)
    return pl.pallas_call(
        kern,
        grid=(B, n_pairs),
        in_specs=[
            pl.BlockSpec((None, S, LANES), lambda b, p: (b, 0, p)),
            pl.BlockSpec((None, S, LANES), lambda b, p: (b, 0, n_pairs + p)),
            pl.BlockSpec((None, S, LANES), lambda b, p: (b, 0, 2 * n_pairs + p)),
        ],
        out_specs=pl.BlockSpec((None, S, LANES), lambda b, p: (b, 0, p)),
        out_shape=jax.ShapeDtypeStruct((B, S, ATT_WIDTH), F32),
        scratch_shapes=[
            pltpu.VMEM((LANES, LANES), BF16),
            pltpu.VMEM((LANES, LANES), BF16),
            pltpu.VMEM((2, S, LANES), BF16),
            pltpu.VMEM((2, 2, blk, nb * blk), F32),
            pltpu.VMEM((2, 2, blk, nb * blk), BF16),
        ],
        compiler_params=_cparams("arbitrary", "arbitrary"),
        name="moba_attn",
    )(qkv, qkv, qkv)


POOL_HALO_STEPS = max(POOL_WINDOWS)


def _pool_kernel(u_ref, w_ref, scale_ref, o_ref, ext_scr, *, steps, nbatch):
    c = pl.program_id(0)
    R = steps * nbatch
    H = POOL_HALO_STEPS * nbatch

    @pl.when(c == 0)
    def _():
        ext_scr[0:H, :] = jnp.zeros((H, POOL_WIDTH), F32)

    @pl.when(c > 0)
    def _():
        ext_scr[0:H, :] = ext_scr[R:R + H, :]

    ext_scr[H:H + R, :] = u_ref[...].astype(F32)

    t_glob = c * steps + lax.broadcasted_iota(jnp.int32, (R, 1), 0) // nbatch
    outs = []
    for gi, w in enumerate(POOL_WINDOWS):
        cols = slice(gi * POOL_GROUP, (gi + 1) * POOL_GROUP)
        u = ext_scr[H:H + R, cols]
        acc = u
        for k in range(1, w):
            acc = acc + ext_scr[H - k * nbatch:H - k * nbatch + R, cols]
        count = jnp.minimum(t_glob + 1, w).astype(F32)
        d = acc / count - u
        outs.append(jnp.dot(d.astype(BF16), w_ref[gi], preferred_element_type=F32))
    y = jnp.concatenate(outs, axis=1) * scale_ref[...]
    o_ref[...] = y.astype(o_ref.dtype)


def _pool(z, w_pool, pool_scale, nbatch):
    T = z.shape[0]
    steps = 64
    R = steps * nbatch
    H = POOL_HALO_STEPS * nbatch
    kern = functools.partial(_pool_kernel, steps=steps, nbatch=nbatch)
    return pl.pallas_call(
        kern,
        grid=(T // R,),
        in_specs=[
            pl.BlockSpec((R, 512), lambda c: (c, COL_POOL)),
            pl.BlockSpec((len(POOL_WINDOWS), POOL_GROUP, POOL_GROUP), lambda c: (0, 0, 0)),
            pl.BlockSpec((1, POOL_WIDTH), lambda c: (0, 0)),
        ],
        out_specs=pl.BlockSpec((R, POOL_WIDTH), lambda c: (c, 0)),
        out_shape=jax.ShapeDtypeStruct((T, POOL_WIDTH), BF16),
        scratch_shapes=[pltpu.VMEM((H + R, POOL_WIDTH), F32)],
        compiler_params=_cparams("arbitrary"),
        name="pool_mixer",
    )(z, w_pool, pool_scale)


def _s5_prep_kernel(ar_ref, ai_ref, ldt_ref, br_ref, bi_ref, lr_ref, li_ref, bbr_ref, bbi_ref):
    ar, ai = ar_ref[...], ai_ref[...]
    dt = jnp.exp(ldt_ref[...])
    mag = jnp.exp(ar * dt)
    lr = mag * jnp.cos(ai * dt)
    li = mag * jnp.sin(ai * dt)
    den = ar * ar + ai * ai
    nr = lr - 1.0
    fr = (nr * ar + li * ai) / den
    fi = (li * ar - nr * ai) / den
    br, bi = br_ref[...], bi_ref[...]
    lr_ref[...] = lr
    li_ref[...] = li
    bbr_ref[...] = fr * br - fi * bi
    bbi_ref[...] = fr * bi + fi * br


def _s5_prep(a_re, a_im, log_dt, b_re, b_im):
    L, G, N = a_re.shape
    P = b_re.shape[-1]
    rows, cols = L * G, N * P
    exp_n = lambda a: jnp.broadcast_to(a[..., None], (L, G, N, P)).reshape(rows, cols)
    ldt = jnp.broadcast_to(log_dt[..., None, None], (L, G, N, P)).reshape(rows, cols)
    shp = jax.ShapeDtypeStruct((rows, cols), F32)
    spec = pl.BlockSpec((rows, cols), lambda: (0, 0))
    lr, li, bbr, bbi = pl.pallas_call(
        _s5_prep_kernel,
        in_specs=[spec] * 5,
        out_specs=[spec] * 4,
        out_shape=[shp] * 4,
        name="s5_discretise",
    )(exp_n(a_re), exp_n(a_im), ldt, b_re.reshape(rows, cols), b_im.reshape(rows, cols))
    lr = lr.reshape(L, G, N, P)[..., 0].reshape(L, G * N)
    li = li.reshape(L, G, N, P)[..., 0].reshape(L, G * N)
    return lr, li, bbr.reshape(L, G, N, P), bbi.reshape(L, G, N, P)


S5_GPB = LANES // SSM_GROUP
S5_NBLK = SSM_WIDTH // LANES
S5_SB = S5_GPB * SSM_STATE
S5_NS = SSM_GROUPS * SSM_STATE


def _s5_layout(lr, li, bbr, bbi, c_re, c_im, nbatch):
    eye = jnp.eye(S5_GPB, dtype=F32)

    def bmat(bb):
        bb = bb.reshape(S5_NBLK, S5_GPB, SSM_STATE, SSM_GROUP)
        m = jnp.einsum('jgnp,gh->jgphn', bb, eye)
        return m.reshape(S5_NBLK, LANES, S5_SB)

    def cmat(c):
        c = c.reshape(S5_NBLK, S5_GPB, SSM_GROUP, SSM_STATE)
        m = jnp.einsum('jgpn,gh->jgnhp', c, eye)
        return m.reshape(S5_NBLK, S5_SB, LANES)

    b_mat = jnp.concatenate([bmat(bbr), bmat(bbi)], axis=2).astype(BF16)
    c_mat = jnp.concatenate([cmat(c_re), -cmat(c_im)], axis=1).astype(BF16)
    lam = jnp.concatenate([jnp.broadcast_to(lr[None, :], (nbatch, S5_NS)),
                           jnp.broadcast_to(li[None, :], (nbatch, S5_NS))], axis=0)
    return b_mat, c_mat, lam


S5_SCAN_COLS = 1024


def _s5_kernel(u_ref, bmat_ref, lam_ref, cmat_ref, d_ref, wglu_ref, bglu_ref, o_ref,
               x_scr, h_scr, *, steps, nbatch):
    @pl.when(pl.program_id(0) == 0)
    def _():
        h_scr[...] = jnp.zeros_like(h_scr)

    u = u_ref[...]
    for j in range(S5_NBLK):
        xj = jnp.dot(u[:, j * LANES:(j + 1) * LANES], bmat_ref[j], preferred_element_type=F32)
        x_scr[:, j * S5_SB:(j + 1) * S5_SB] = xj[:, :S5_SB]
        x_scr[:, S5_NS + j * S5_SB:S5_NS + (j + 1) * S5_SB] = xj[:, S5_SB:]

    for c in range(S5_NS // S5_SCAN_COLS):
        re = slice(c * S5_SCAN_COLS, (c + 1) * S5_SCAN_COLS)
        im = slice(S5_NS + c * S5_SCAN_COLS, S5_NS + (c + 1) * S5_SCAN_COLS)
        lr = lam_ref[0:nbatch, re]
        li = lam_ref[nbatch:2 * nbatch, re]

        def body(t, carry, re=re, im=im, lr=lr, li=li):
            hr, hi = carry
            r0 = pl.multiple_of(t * nbatch, nbatch)
            nhr = lr * hr - li * hi + x_scr[pl.ds(r0, nbatch), re]
            nhi = lr * hi + li * hr + x_scr[pl.ds(r0, nbatch), im]
            x_scr[pl.ds(r0, nbatch), re] = nhr
            x_scr[pl.ds(r0, nbatch), im] = nhi
            return nhr, nhi

        hr, hi = lax.fori_loop(0, steps, body, (h_scr[:, re], h_scr[:, im]), unroll=4)
        h_scr[:, re] = hr
        h_scr[:, im] = hi

    ys = []
    for j in range(S5_NBLK):
        hr = x_scr[:, j * S5_SB:(j + 1) * S5_SB].astype(BF16)
        hi = x_scr[:, S5_NS + j * S5_SB:S5_NS + (j + 1) * S5_SB].astype(BF16)
        ys.append(jnp.dot(hr, cmat_ref[j, 0:S5_SB, :], preferred_element_type=F32)
                  + jnp.dot(hi, cmat_ref[j, S5_SB:2 * S5_SB, :], preferred_element_type=F32))
    y = jnp.concatenate(ys, axis=1) + d_ref[...] * u.astype(F32)
    y = jax.nn.gelu(y, approximate=True)
    z = jnp.dot(y.astype(BF16), wglu_ref[...], preferred_element_type=F32) + bglu_ref[...]
    o_ref[...] = (z[:, :SSM_WIDTH] * jax.nn.sigmoid(z[:, SSM_WIDTH:])).astype(o_ref.dtype)


def _s5(z, b_mat, lam, c_mat, d_skip, w_glu, b_glu, nbatch):
    T = z.shape[0]
    steps = 64
    R = steps * nbatch
    kern = functools.partial(_s5_kernel, steps=steps, nbatch=nbatch)
    const = lambda *shape: pl.BlockSpec(shape, lambda c: (0,) * len(shape))
    return pl.pallas_call(
        kern,
        grid=(T // R,),
        in_specs=[
            pl.BlockSpec((R, 512), lambda c: (c, COL_SSM)),
            const(S5_NBLK, LANES, 2 * S5_SB),
            const(2 * nbatch, S5_NS),
            const(S5_NBLK, 2 * S5_SB, LANES),
            const(1, SSM_WIDTH),
            const(SSM_WIDTH, 2 * SSM_WIDTH),
            const(1, 2 * SSM_WIDTH),
        ],
        out_specs=pl.BlockSpec((R, SSM_WIDTH), lambda c: (c, 0)),
        out_shape=jax.ShapeDtypeStruct((T, SSM_WIDTH), BF16),
        scratch_shapes=[pltpu.VMEM((R, 2 * S5_NS), F32), pltpu.VMEM((nbatch, 2 * S5_NS), F32)],
        compiler_params=_cparams("arbitrary"),
        name="s5_mixer",
    )(z, b_mat, lam, c_mat, d_skip, w_glu, b_glu)


def _merge_kernel(h_ref, ya_ref, yb_ref, yc_ref, g0a, g0b, g1a, g1b, g2a, g2b,
                  wbr_ref, wout_ref, lng_ref, lnb_ref, o_ref, yb_scr):
    nbatch, steps, _ = yb_ref.shape
    nslab = yb_scr.shape[0]
    for b in range(nbatch):
        for c in range(nslab):
            yb_scr[c, pl.ds(b, steps, stride=nbatch), :] = yb_ref[b, :, c * LANES:(c + 1) * LANES]
    yb = jnp.concatenate([yb_scr[c] for c in range(nslab)], axis=1).astype(BF16)
    ys = (ya_ref[...], yb, yc_ref[...])
    gs = ((g0a, g0b), (g1a, g1b), (g2a, g2b))
    half = D_MODEL // 2
    merged = []
    for hcol in range(2):
        acc = None
        for n in range(N_BRANCH):
            proj = jnp.dot(ys[n], wbr_ref[n, :, hcol * half:(hcol + 1) * half],
                           preferred_element_type=F32)
            term = jax.nn.sigmoid(gs[n][hcol][...].astype(F32)) * proj
            acc = term if acc is None else acc + term
        merged.append(acc.astype(BF16))
    merged = jnp.concatenate(merged, axis=1)
    mix = jnp.dot(merged, wout_ref[...], preferred_element_type=F32)
    r = DN_ALPHA * h_ref[...] + mix
    o_ref[...] = _layer_norm(r, lng_ref[...], lnb_ref[...])


def _merge(h, y_a, y_b, y_c, z, w_branch, w_out, ln_g, ln_b):
    T = h.shape[0]
    nbatch = y_b.shape[0]
    tm = 512
    steps = tm // nbatch
    row = lambda w: pl.BlockSpec((tm, w), lambda i: (i, 0))
    gate = lambda k: pl.BlockSpec((tm, 512), lambda i, k=k: (i, COL_GATE + k))
    return pl.pallas_call(
        _merge_kernel,
        grid=(T // tm,),
        in_specs=[row(D_MODEL), row(512),
                  pl.BlockSpec((nbatch, steps, ATT_WIDTH), lambda i: (0, i, 0)),
                  row(512)]
        + [gate(k) for k in range(2 * N_BRANCH)]
        + [
            pl.BlockSpec((N_BRANCH, BRANCH_WIDTH, D_MODEL), lambda i: (0, 0, 0)),
            pl.BlockSpec((D_MODEL, D_MODEL), lambda i: (0, 0)),
            pl.BlockSpec((1, D_MODEL), lambda i: (0, 0)),
            pl.BlockSpec((1, D_MODEL), lambda i: (0, 0)),
        ],
        out_specs=row(D_MODEL),
        out_shape=jax.ShapeDtypeStruct((T, D_MODEL), F32),
        scratch_shapes=[pltpu.VMEM((ATT_WIDTH // LANES, tm, LANES), F32)],
        compiler_params=_cparams("arbitrary"),
        name="merge_ln",
    )(h, y_a, y_b, y_c, z, z, z, z, z, z, w_branch, w_out, ln_g, ln_b)


def _ffn_kernel(x_ref, w1_ref, w3_ref, w2_ref, lng_ref, lnb_ref, o_ref, xb_ref, acc_ref):
    f = pl.program_id(1)

    @pl.when(f == 0)
    def _():
        xb_ref[...] = x_ref[...].astype(BF16)
        acc_ref[...] = jnp.zeros_like(acc_ref)

    xb = xb_ref[...]
    a = jax.nn.silu(jnp.dot(xb, w1_ref[...], preferred_element_type=F32))
    a = a * jnp.dot(xb, w3_ref[...], preferred_element_type=F32)
    acc_ref[...] += jnp.dot(a.astype(BF16), w2_ref[...], preferred_element_type=F32)

    @pl.when(f == pl.num_programs(1) - 1)
    def _():
        r = DN_ALPHA * x_ref[...] + acc_ref[...]
        o_ref[...] = _layer_norm(r, lng_ref[...], lnb_ref[...])


FFN_TF = 1408


def _ffn(x, w1, w3, w2, ln_g, ln_b):
    T = x.shape[0]
    tm, tf = 512, FFN_TF
    return pl.pallas_call(
        _ffn_kernel,
        grid=(T // tm, D_FF // tf),
        in_specs=[
            pl.BlockSpec((tm, D_MODEL), lambda i, f: (i, 0)),
            pl.BlockSpec((D_MODEL, tf), lambda i, f: (0, f)),
            pl.BlockSpec((D_MODEL, tf), lambda i, f: (0, f)),
            pl.BlockSpec((tf, D_MODEL), lambda i, f: (f, 0)),
            pl.BlockSpec((1, D_MODEL), lambda i, f: (0, 0)),
            pl.BlockSpec((1, D_MODEL), lambda i, f: (0, 0)),
        ],
        out_specs=pl.BlockSpec((tm, D_MODEL), lambda i, f: (i, 0)),
        out_shape=jax.ShapeDtypeStruct((T, D_MODEL), F32),
        scratch_shapes=[pltpu.VMEM((tm, D_MODEL), BF16), pltpu.VMEM((tm, D_MODEL), F32)],
        compiler_params=_cparams("arbitrary", "arbitrary"),
        name="ffn_ln",
    )(x, w1, w3, w2, ln_g, ln_b)


def _router_kernel(x_ref, w_ref, b_ref, idx_ref, wgt_ref):
    logits = jnp.dot(x_ref[...], w_ref[...], preferred_element_type=F32,
                     precision=lax.Precision.HIGHEST) + b_ref[...]
    lane = lax.broadcasted_iota(jnp.int32, logits.shape, 1)
    l1 = jnp.where(lane < N_EXPERTS, logits, -jnp.inf)
    m1 = jnp.max(l1, axis=1, keepdims=True)
    i1 = jnp.min(jnp.where(l1 == m1, lane, LANES), axis=1, keepdims=True)
    l2 = jnp.where(lane == i1, -jnp.inf, l1)
    m2 = jnp.max(l2, axis=1, keepdims=True)
    i2 = jnp.min(jnp.where(l2 == m2, lane, LANES), axis=1, keepdims=True)
    e = jnp.exp(m2 - m1)
    w_top = 1.0 / (1.0 + e)
    w_sec = e / (1.0 + e)
    idx_ref[...] = jnp.where(lane == 0, i1, jnp.where(lane == 1, i2, 0))
    wgt_ref[...] = jnp.where(lane == 0, w_top, jnp.where(lane == 1, w_sec, 0.0))


def _router(x, w, b):
    T = x.shape[0]
    tm = 1024
    out = pl.BlockSpec((tm, LANES), lambda i: (i, 0))
    return pl.pallas_call(
        _router_kernel,
        grid=(T // tm,),
        in_specs=[
            pl.BlockSpec((tm, D_MODEL), lambda i: (i, 0)),
            pl.BlockSpec((D_MODEL, LANES), lambda i: (0, 0)),
            pl.BlockSpec((1, LANES), lambda i: (0, 0)),
        ],
        out_specs=[out, out],
        out_shape=[jax.ShapeDtypeStruct((T, LANES), jnp.int32),
                   jax.ShapeDtypeStruct((T, LANES), F32)],
        compiler_params=_cparams("arbitrary"),
        name="moe_router",
    )(x, w, b)


MOE_TM = 512
MOE_DMA_UNROLL = 8


def _route_plan(idx, wgt, T):
    n_assign = TOP_K * T
    n_tiles = n_assign // MOE_TM + N_EXPERTS
    e_flat = idx.T.reshape(-1)
    w_flat = wgt.T.reshape(-1)
    order = jnp.argsort(e_flat, stable=True).astype(jnp.int32)
    counts = jnp.sum((e_flat[:, None] == jnp.arange(N_EXPERTS)[None, :]).astype(jnp.int32), axis=0)
    group_start = jnp.cumsum(counts) - counts
    tiles_per = (counts + MOE_TM - 1) // MOE_TM
    tile_end = jnp.cumsum(tiles_per)
    tile_start = tile_end - tiles_per
    tile_id = jnp.arange(n_tiles, dtype=jnp.int32)
    used = tile_id < tile_end[-1]
    tile_expert = jnp.minimum(jnp.sum((tile_id[:, None] >= tile_end[None, :]).astype(jnp.int32), axis=1),
                              N_EXPERTS - 1)
    tile_first = (tile_id - tile_start[tile_expert]) * MOE_TM
    tile_rows = jnp.where(used, jnp.clip(counts[tile_expert] - tile_first, 0, MOE_TM), 0)
    last_expert = tile_expert[jnp.maximum(tile_end[-1] - 1, 0)]
    tile_expert = jnp.where(used, tile_expert, last_expert).astype(jnp.int32)
    j = jnp.arange(MOE_TM, dtype=jnp.int32)[None, :]
    valid = j < tile_rows[:, None]
    pos = jnp.clip(group_start[tile_expert][:, None] + tile_first[:, None] + j, 0, n_assign - 1)
    a = jnp.where(valid, order[pos], 0)
    src_tok = jnp.where(a >= T, a - T, a).astype(jnp.int32)
    row_w = jnp.where(valid, w_flat[a], 0.0)
    return (tile_expert, tile_rows.astype(jnp.int32), src_tok[:, None, :], a.astype(jnp.int32)[:, None, :],
            row_w[:, :, None])


def _moe_kernel(te_ref, tn_ref, src_ref, src_next_ref, dst_ref, roww_ref, x_hbm, w1_ref, w3_ref, w2_ref,
                buf_hbm, xg, xb, acc, y_scr, gsem, ssem):
    t = pl.program_id(0)
    f = pl.program_id(1)
    n_tiles = pl.num_programs(0)
    last_f = pl.num_programs(1) - 1
    nrows = tn_ref[t]
    slot = t % 2

    def start_gather(idx_ref, buf_slot, r):
        pltpu.make_async_copy(x_hbm.at[pl.ds(idx_ref[0, r], 1), :], xg.at[buf_slot, pl.ds(r, 1), :],
                              gsem.at[buf_slot]).start()

    def start_scatter(r):
        pltpu.make_async_copy(y_scr.at[pl.ds(r, 1), :], buf_hbm.at[pl.ds(dst_ref[0, r], 1), :], ssem).start()

    def for_rows(n, fn):
        ngroups = n // MOE_DMA_UNROLL

        def group(g, c):
            base = pl.multiple_of(g * MOE_DMA_UNROLL, MOE_DMA_UNROLL)
            for u in range(MOE_DMA_UNROLL):
                fn(base + u)
            return c

        def single(r, c):
            fn(r)
            return c

        lax.fori_loop(0, ngroups, group, 0)
        lax.fori_loop(ngroups * MOE_DMA_UNROLL, n, single, 0)

    def wait_rows(n, src, dst, sem):
        p = MOE_TM
        while p >= 1:
            @pl.when((n & p) != 0)
            def _(p=p):
                pltpu.make_async_copy(src.at[pl.ds(0, p), :], dst.at[pl.ds(0, p), :], sem).wait()
            p //= 2

    @pl.when((t == 0) & (f == 0))
    def _():
        xg[...] = jnp.zeros_like(xg)
        for_rows(nrows, functools.partial(start_gather, src_ref, 0))

    @pl.when(f == 0)
    def _():
        wait_rows(nrows, x_hbm, xg.at[slot], gsem.at[slot])

        @pl.when(nrows > 0)
        def _():
            xb[...] = xg[slot].astype(BF16)

        @pl.when(t + 1 < n_tiles)
        def _():
            for_rows(tn_ref[t + 1], functools.partial(start_gather, src_next_ref, 1 - slot))

    @pl.when(nrows > 0)
    def _():
        x = xb[...]
        a = jax.nn.silu(jnp.dot(x, w1_ref[...], preferred_element_type=F32))
        a = a * jnp.dot(x, w3_ref[...], preferred_element_type=F32)
        y = jnp.dot(a.astype(BF16), w2_ref[...], preferred_element_type=F32)

        @pl.when(f == 0)
        def _():
            acc[...] = y

        @pl.when(f > 0)
        def _():
            acc[...] += y

    @pl.when(f == last_f)
    def _():
        @pl.when(t > 0)
        def _():
            wait_rows(tn_ref[t - 1], y_scr, buf_hbm, ssem)

        @pl.when(nrows > 0)
        def _():
            y_scr[...] = acc[...] * roww_ref[...]
            for_rows(nrows, start_scatter)

        @pl.when(t == n_tiles - 1)
        def _():
            wait_rows(nrows, y_scr, buf_hbm, ssem)


def _moe_experts(x, plan, w1, w3, w2):
    T = x.shape[0]
    tile_expert, tile_rows, src_tok, dst_row, row_w = plan
    n_tiles = tile_expert.shape[0]
    tf = FFN_TF
    smem_rows = pl.BlockSpec((None, 1, MOE_TM), lambda t, f, te, tn: (t, 0, 0), memory_space=pltpu.SMEM)
    smem_next = pl.BlockSpec((None, 1, MOE_TM), lambda t, f, te, tn: (jnp.minimum(t + 1, n_tiles - 1), 0, 0),
                             memory_space=pltpu.SMEM)
    grid_spec = pltpu.PrefetchScalarGridSpec(
        num_scalar_prefetch=2,
        grid=(n_tiles, D_FF // tf),
        in_specs=[
            smem_rows,
            smem_next,
            smem_rows,
            pl.BlockSpec((None, MOE_TM, 1), lambda t, f, te, tn: (t, 0, 0)),
            pl.BlockSpec(memory_space=pl.ANY),
            pl.BlockSpec((None, D_MODEL, tf), lambda t, f, te, tn: (te[t], 0, f)),
            pl.BlockSpec((None, D_MODEL, tf), lambda t, f, te, tn: (te[t], 0, f)),
            pl.BlockSpec((None, tf, D_MODEL), lambda t, f, te, tn: (te[t], f, 0)),
        ],
        out_specs=pl.BlockSpec(memory_space=pl.ANY),
        scratch_shapes=[
            pltpu.VMEM((2, MOE_TM, D_MODEL), F32),
            pltpu.VMEM((MOE_TM, D_MODEL), BF16),
            pltpu.VMEM((MOE_TM, D_MODEL), F32),
            pltpu.VMEM((MOE_TM, D_MODEL), F32),
            pltpu.SemaphoreType.DMA((2,)),
            pltpu.SemaphoreType.DMA(()),
        ],
    )
    return pl.pallas_call(
        _moe_kernel,
        grid_spec=grid_spec,
        out_shape=jax.ShapeDtypeStruct((TOP_K * T, D_MODEL), F32),
        compiler_params=_cparams("arbitrary", "arbitrary"),
        name="moe_experts",
    )(tile_expert, tile_rows, src_tok, src_tok, dst_row, row_w, x, w1, w3, w2)


def _moe_combine_kernel(x_ref, y0_ref, y1_ref, lng_ref, lnb_ref, o_ref):
    r = DN_ALPHA * x_ref[...] + (y0_ref[...] + y1_ref[...])
    o_ref[...] = _layer_norm(r, lng_ref[...], lnb_ref[...])


def _moe_combine(x, buf, ln_g, ln_b):
    T = x.shape[0]
    tm = 1024
    nblk = T // tm
    return pl.pallas_call(
        _moe_combine_kernel,
        grid=(nblk,),
        in_specs=[
            pl.BlockSpec((tm, D_MODEL), lambda i: (i, 0)),
            pl.BlockSpec((tm, D_MODEL), lambda i: (i, 0)),
            pl.BlockSpec((tm, D_MODEL), lambda i: (i + nblk, 0)),
            pl.BlockSpec((1, D_MODEL), lambda i: (0, 0)),
            pl.BlockSpec((1, D_MODEL), lambda i: (0, 0)),
        ],
        out_specs=pl.BlockSpec((tm, D_MODEL), lambda i: (i, 0)),
        out_shape=jax.ShapeDtypeStruct((T, D_MODEL), F32),
        compiler_params=_cparams("arbitrary"),
        name="moe_combine_ln",
    )(x, buf, buf, ln_g, ln_b)


def kernel(x, ln1_g, ln1_b, w_in, b_in, w_pool, pool_scale, ssm_a_re, ssm_a_im, ssm_log_dt,
           ssm_b_re, ssm_b_im, ssm_c_re, ssm_c_im, ssm_d, w_glu, b_glu, w_branch, w_out,
           ln2_g, ln2_b, ffn_w1, ffn_w3, ffn_w2, moe_router, moe_router_b, moe_w1, moe_w3, moe_w2):
    B, S, D = x.shape
    T = B * S
    assert B == SUBLANES and D == D_MODEL
    h = x.transpose(1, 0, 2).reshape(T, D)

    lr, li, bbr, bbi = _s5_prep(ssm_a_re, ssm_a_im, ssm_log_dt, ssm_b_re, ssm_b_im)
    row = lambda a: a.reshape(1, -1)

    for i in range(DEPTH):
        qkv, z = _inproj(h, _to_bf16(w_in, (i,)), row(b_in[i]), B)
        y_b = _attention(qkv)
        y_a = _pool(z, w_pool[i].astype(BF16), row(pool_scale[i]), B)
        b_mat, c_mat, lam = _s5_layout(lr[i], li[i], bbr[i], bbi[i], ssm_c_re[i], ssm_c_im[i], B)
        y_c = _s5(z, b_mat, lam, c_mat, row(ssm_d[i]), w_glu[i].astype(BF16), row(b_glu[i]), B)
        h = _merge(h, y_a, y_b, y_c, z, w_branch[i].astype(BF16), w_out[i].astype(BF16),
                   row(ln1_g[i]), row(ln1_b[i]))
        j = i // 2
        if i % 2 == 0:
            h = _ffn(h, _to_bf16(ffn_w1, (j,)), _to_bf16(ffn_w3, (j,)), _to_bf16(ffn_w2, (j,)),
                     row(ln2_g[i]), row(ln2_b[i]))
        else:
            rw = jnp.pad(moe_router[j], ((0, 0), (0, LANES - N_EXPERTS)))
            rb = jnp.pad(moe_router_b[j], (0, LANES - N_EXPERTS)).reshape(1, LANES)
            idx, wgt = _router(h, rw, rb)
            plan = _route_plan(idx[:, :TOP_K], wgt[:, :TOP_K], T)
            buf = _moe_experts(h, plan, _to_bf16(moe_w1, (j,)), _to_bf16(moe_w3, (j,)),
                               _to_bf16(moe_w2, (j,)))
            h = _moe_combine(h, buf, row(ln2_g[i]), row(ln2_b[i]))
    return h.reshape(S, B, D).transpose(1, 0, 2)
```

```python
import functools

import jax
import jax.numpy as jnp
from jax import lax
from jax.experimental import pallas as pl
from jax.experimental.pallas import tpu as pltpu

F32 = jnp.float32
BF16 = jnp.bfloat16

D_MODEL = 1024
DEPTH = 4
POOL_WINDOWS = (2, 4, 8, 16)
POOL_WIDTH = 512
POOL_GROUP = POOL_WIDTH // len(POOL_WINDOWS)
ATT_HEADS = 8
HEAD_DIM = 64
ATT_WIDTH = ATT_HEADS * HEAD_DIM
MOBA_BLOCK = 256
MOBA_TOPK = 3
SSM_WIDTH = 512
SSM_GROUP = 16
SSM_GROUPS = SSM_WIDTH // SSM_GROUP
SSM_STATE = 64
N_BRANCH = 3
BRANCH_WIDTH = 512
IN_WIDTH = POOL_WIDTH + 3 * ATT_WIDTH + SSM_WIDTH + N_BRANCH * D_MODEL
D_FF = 2816
N_EXPERTS = 8
TOP_K = 2
DN_ALPHA = (2 * DEPTH) ** 0.25
LN_EPS = 1e-5

LANES = 128
SUBLANES = 8
VMEM_LIMIT = 48 * 1024 * 1024

NEG = -1e30

COL_POOL, COL_SSM, COL_GATE = 0, 1, 2
QKV_WIDTH = 3 * ATT_WIDTH
Z_WIDTH = IN_WIDTH - QKV_WIDTH
N_QKV_BLOCKS = QKV_WIDTH // 512


def _cparams(*sem):
    return pltpu.CompilerParams(dimension_semantics=sem, vmem_limit_bytes=VMEM_LIMIT)


def _layer_norm(r, g, b):
    mu = jnp.mean(r, axis=-1, keepdims=True)
    c = r - mu
    var = jnp.mean(c * c, axis=-1, keepdims=True)
    return c * lax.rsqrt(var + LN_EPS) * g + b


CAST_BLOCK_BYTES = 6 * 1024 * 1024


def _cast_kernel(w_ref, o_ref):
    o_ref[...] = w_ref[...].astype(o_ref.dtype)


def _to_bf16(w, lead):
    *outer, R, C = w.shape
    free = outer[len(lead):]
    rb = R
    while rb * C * 4 > CAST_BLOCK_BYTES and rb % 2 == 0 and (rb // 2) % SUBLANES == 0:
        rb //= 2
    nfree = len(free)
    squeeze = (None,) * len(outer)
    in_map = lambda *g: (*lead, *g[:nfree], g[nfree], 0)
    out_map = lambda *g: (*g[:nfree], g[nfree], 0)
    return pl.pallas_call(
        _cast_kernel,
        grid=(*free, R // rb),
        in_specs=[pl.BlockSpec((*squeeze, rb, C), in_map)],
        out_specs=pl.BlockSpec(((None,) * nfree) + (rb, C), out_map),
        out_shape=jax.ShapeDtypeStruct((*free, R, C), BF16),
        compiler_params=_cparams(*(("arbitrary",) * (nfree + 1))),
        name="cast_bf16",
    )(w)


def _inproj_kernel(x_ref, w_ref, b_ref, qkv_ref, z_ref, xb_ref, acc_scr, *, nbatch):
    j = pl.program_id(1)

    @pl.when(j == 0)
    def _():
        xb_ref[...] = x_ref[...].astype(BF16)

    acc = jnp.dot(xb_ref[...], w_ref[...], preferred_element_type=F32) + b_ref[...]

    @pl.when(j < N_QKV_BLOCKS)
    def _():
        nslab, rows, _ = acc_scr.shape
        steps = rows // nbatch
        for c in range(nslab):
            acc_scr[c] = acc[:, c * LANES:(c + 1) * LANES]
        for b in range(nbatch):
            for c in range(nslab):
                qkv_ref[b, :, c * LANES:(c + 1) * LANES] = (
                    acc_scr[c, pl.ds(b, steps, stride=nbatch), :].astype(qkv_ref.dtype))

    @pl.when(j >= N_QKV_BLOCKS)
    def _():
        z_ref[...] = acc.astype(z_ref.dtype)


def _qkv_first(j):
    return jnp.where(j < N_QKV_BLOCKS, j + 1, jnp.where(j == N_QKV_BLOCKS, 0, j))


def _inproj(x, w, b, nbatch):
    T, K = x.shape
    N = w.shape[1]
    S = T // nbatch
    tm, tn = 2048, 512
    steps = tm // nbatch
    kern = functools.partial(_inproj_kernel, nbatch=nbatch)
    return pl.pallas_call(
        kern,
        grid=(T // tm, N // tn),
        in_specs=[
            pl.BlockSpec((tm, K), lambda i, j: (i, 0)),
            pl.BlockSpec((K, tn), lambda i, j: (0, _qkv_first(j))),
            pl.BlockSpec((1, tn), lambda i, j: (0, _qkv_first(j))),
        ],
        out_specs=[
            pl.BlockSpec((nbatch, steps, tn), lambda i, j: (0, i, jnp.minimum(j, N_QKV_BLOCKS - 1))),
            pl.BlockSpec((tm, tn), lambda i, j: (i, jnp.maximum(j - N_QKV_BLOCKS, 0))),
        ],
        out_shape=[jax.ShapeDtypeStruct((nbatch, S, QKV_WIDTH), BF16),
                   jax.ShapeDtypeStruct((T, Z_WIDTH), BF16)],
        scratch_shapes=[pltpu.VMEM((tm, K), BF16), pltpu.VMEM((tn // LANES, tm, LANES), F32)],
        compiler_params=_cparams("arbitrary", "arbitrary"),
        name="inproj",
    )(x, w, b)


def _attn_block(i, q_ref, k_ref, v_ref, o_ref, kmh_ref, kml_ref, kaug_ref, s_scr, p_scr,
                *, nb, blk, dh, n_heads, topk):
    pair = pl.program_id(1)
    W = 2 * dh
    lane = lax.broadcasted_iota(jnp.int32, (1, W), 1)
    head_of_lane = lane // dh
    nt = (((1,), (1,)), ((), ()))

    if i == 0:
        kmh_ref[...] = jnp.zeros_like(kmh_ref)
        kml_ref[...] = jnp.zeros_like(kml_ref)
        for n in range(nb):
            kb = k_ref[n * blk:(n + 1) * blk, :].astype(F32)
            mean = jnp.sum(kb, axis=0, keepdims=True) * (1.0 / blk)
            for hl in range(2):
                row = jnp.where(head_of_lane == hl, mean, 0.0)
                hi = row.astype(BF16)
                lo = (row - hi.astype(F32)).astype(BF16)
                kmh_ref[hl * nb + n:hl * nb + n + 1, :] = hi
                kml_ref[hl * nb + n:hl * nb + n + 1, :] = lo

    par = i % 2
    q = q_ref[i * blk:(i + 1) * blk, :] * jnp.asarray(dh ** -0.5, BF16)
    slopes = []
    for hl in range(2):
        h_f = (pair * 2 + hl + 1).astype(F32)
        slopes.append(jnp.exp2(jnp.full((1, blk), -8.0 / n_heads, F32) * h_f))

    ones = jnp.where((lane == 0) | (lane == dh), 1.0, 0.0)
    if i > 0:
        gate_t = (lax.dot_general(kmh_ref[...], q, nt, preferred_element_type=F32)
                  + lax.dot_general(kml_ref[...], q, nt, preferred_element_type=F32))
        row = lax.broadcasted_iota(jnp.int32, (nb, blk), 0)
        rowdist = ((row - i) * blk).astype(F32)
        parts = []
        for hl in range(2):
            g = gate_t[hl * nb:(hl + 1) * nb, :]
            cnt = jnp.zeros((nb, blk), F32)
            for n2 in range(i):
                r = g[n2:n2 + 1, :]
                cnt = cnt + jnp.where(r > g, 1.0,
                                      jnp.where(r == g, jnp.where(n2 < row, 1.0, 0.0), 0.0))
            sel = jnp.where(cnt < float(topk), 1.0, 0.0)
            parts.append(jnp.where(row < i, jnp.where(sel > 0.5, slopes[hl] * rowdist, NEG), 0.0))
        zrows = lambda n: jnp.zeros((n, blk), F32)
        bias_t = jnp.concatenate([zrows(nb), parts[1], zrows(dh - 2 * nb), zrows(nb), parts[0],
                                  zrows(dh - 2 * nb)], axis=0)
        aug = (bias_t.T + ones).astype(BF16)
    else:
        aug = jnp.broadcast_to(ones, (blk, W)).astype(BF16)

    r_io = lax.broadcasted_iota(jnp.int32, (blk, blk), 0)
    c_io = lax.broadcasted_iota(jnp.int32, (blk, blk), 1)
    nkeys = (i + 1) * blk

    outs = []
    for hl in range(2):
        qh = jnp.where(head_of_lane == hl, q, aug)
        macc = None
        for n in range(i + 1):
            s = lax.dot_general(qh, kaug_ref[hl, n * blk:(n + 1) * blk, :], nt,
                                preferred_element_type=F32)
            if n == i:
                s = jnp.where(c_io <= r_io, s, NEG)
            s_scr[par, hl, :, n * blk:(n + 1) * blk] = s
            t = jnp.maximum(s[:, :LANES], s[:, LANES:])
            macc = t if macc is None else jnp.maximum(macc, t)
        m = jnp.max(macc, axis=1, keepdims=True)
        lacc = None
        for n in range(i + 1):
            p = jnp.exp(s_scr[par, hl, :, n * blk:(n + 1) * blk] - m)
            p_scr[par, hl, :, n * blk:(n + 1) * blk] = p.astype(BF16)
            t = p[:, :LANES] + p[:, LANES:]
            lacc = t if lacc is None else lacc + t
        l = jnp.sum(lacc, axis=1, keepdims=True)
        acc = jnp.dot(p_scr[par, hl, :, 0:nkeys], v_ref[0:nkeys, :], preferred_element_type=F32)
        outs.append(acc * (1.0 / l))

    o_ref[i * blk:(i + 1) * blk, :] = (
        jnp.where(head_of_lane == 0, outs[0], outs[1]).astype(o_ref.dtype))


def _attn_aug_keys(k_ref, kaug_ref, *, nb, blk, dh, n_heads):
    pair = pl.program_id(1)
    lane = lax.broadcasted_iota(jnp.int32, (1, 2 * dh), 1)
    key_pos = lax.broadcasted_iota(jnp.int32, (blk, 1), 0).astype(F32)
    for hl in range(2):
        h_f = (pair * 2 + hl + 1).astype(F32)
        slope = jnp.exp2(jnp.full((blk, 1), -8.0 / n_heads, F32) * h_f)
        base = (1 - hl) * dh
        for n in range(nb):
            extra = jnp.where(lane == base, slope * key_pos,
                              jnp.where(lane == base + nb + n, 1.0, 0.0))
            kaug_ref[hl, n * blk:(n + 1) * blk, :] = jnp.where(
                lane // dh == hl, k_ref[n * blk:(n + 1) * blk, :], extra.astype(BF16))


def _attn_kernel(q_ref, k_ref, v_ref, o_ref, kmh_ref, kml_ref, kaug_ref, s_scr, p_scr,
                 *, nb, blk, dh, n_heads, topk):
    _attn_aug_keys(k_ref, kaug_ref, nb=nb, blk=blk, dh=dh, n_heads=n_heads)
    for i in range(nb):
        _attn_block(i, q_ref, k_ref, v_ref, o_ref, kmh_ref, kml_ref, kaug_ref, s_scr, p_scr,
                    nb=nb, blk=blk, dh=dh, n_heads=n_heads, topk=topk)


def _attention(qkv):
    B, S, _ = qkv.shape
    blk, dh = MOBA_BLOCK, HEAD_DIM
    nb = S // blk
    assert nb * blk == S and nb == SUBLANES and 2 * dh == LANES
    assert 8 % ATT_HEADS == 0 and 2 * nb + 1 <= dh
    n_pairs = ATT_HEADS // 2
    kern = functools.partial(_attn_kernel, nb=nb, blk=blk, dh=dh, n_heads=ATT_HEADS, topk=MOBA_TOPK)
    return pl.pallas_call(
        kern,
        grid=(B, n_pairs),
        in_specs=[
            pl.BlockSpec((None, S, LANES), lambda b, p: (b, 0, p)),
            pl.BlockSpec((None, S, LANES), lambda b, p: (b, 0, n_pairs + p)),
            pl.BlockSpec((None, S, LANES), lambda b, p: (b, 0, 2 * n_pairs + p)),
        ],
        out_specs=pl.BlockSpec((None, S, LANES), lambda b, p: (b, 0, p)),
        out_shape=jax.ShapeDtypeStruct((B, S, ATT_WIDTH), F32),
        scratch_shapes=[
            pltpu.VMEM((LANES, LANES), BF16),
            pltpu.VMEM((LANES, LANES), BF16),
            pltpu.VMEM((2, S, LANES), BF16),
            pltpu.VMEM((2, 2, blk, nb * blk), F32),
            pltpu.VMEM((2, 2, blk, nb * blk), BF16),
        ],
        compiler_params=_cparams("arbitrary", "arbitrary"),
        name="moba_attn",
    )(qkv, qkv, qkv)


POOL_HALO_STEPS = max(POOL_WINDOWS)


def _pool_kernel(u_ref, w_ref, scale_ref, o_ref, ext_scr, *, steps, nbatch):
    c = pl.program_id(0)
    R = steps * nbatch
    H = POOL_HALO_STEPS * nbatch

    @pl.when(c == 0)
    def _():
        ext_scr[0:H, :] = jnp.zeros((H, POOL_WIDTH), F32)

    @pl.when(c > 0)
    def _():
        ext_scr[0:H, :] = ext_scr[R:R + H, :]

    ext_scr[H:H + R, :] = u_ref[...].astype(F32)

    t_glob = c * steps + lax.broadcasted_iota(jnp.int32, (R, 1), 0) // nbatch
    outs = []
    for gi, w in enumerate(POOL_WINDOWS):
        cols = slice(gi * POOL_GROUP, (gi + 1) * POOL_GROUP)
        u = ext_scr[H:H + R, cols]
        acc = u
        for k in range(1, w):
            acc = acc + ext_scr[H - k * nbatch:H - k * nbatch + R, cols]
        count = jnp.minimum(t_glob + 1, w).astype(F32)
        d = acc / count - u
        outs.append(jnp.dot(d.astype(BF16), w_ref[gi], preferred_element_type=F32))
    y = jnp.concatenate(outs, axis=1) * scale_ref[...]
    o_ref[...] = y.astype(o_ref.dtype)


def _pool(z, w_pool, pool_scale, nbatch):
    T = z.shape[0]
    steps = 64
    R = steps * nbatch
    H = POOL_HALO_STEPS * nbatch
    kern = functools.partial(_pool_kernel, steps=steps, nbatch=nbatch)
    return pl.pallas_call(
        kern,
        grid=(T // R,),
        in_specs=[
            pl.BlockSpec((R, 512), lambda c: (c, COL_POOL)),
            pl.BlockSpec((len(POOL_WINDOWS), POOL_GROUP, POOL_GROUP), lambda c: (0, 0, 0)),
            pl.BlockSpec((1, POOL_WIDTH), lambda c: (0, 0)),
        ],
        out_specs=pl.BlockSpec((R, POOL_WIDTH), lambda c: (c, 0)),
        out_shape=jax.ShapeDtypeStruct((T, POOL_WIDTH), BF16),
        scratch_shapes=[pltpu.VMEM((H + R, POOL_WIDTH), F32)],
        compiler_params=_cparams("arbitrary"),
        name="pool_mixer",
    )(z, w_pool, pool_scale)


def _s5_prep_kernel(ar_ref, ai_ref, ldt_ref, br_ref, bi_ref, lr_ref, li_ref, bbr_ref, bbi_ref):
    ar, ai = ar_ref[...], ai_ref[...]
    dt = jnp.exp(ldt_ref[...])
    mag = jnp.exp(ar * dt)
    lr = mag * jnp.cos(ai * dt)
    li = mag * jnp.sin(ai * dt)
    den = ar * ar + ai * ai
    nr = lr - 1.0
    fr = (nr * ar + li * ai) / den
    fi = (li * ar - nr * ai) / den
    br, bi = br_ref[...], bi_ref[...]
    lr_ref[...] = lr
    li_ref[...] = li
    bbr_ref[...] = fr * br - fi * bi
    bbi_ref[...] = fr * bi + fi * br


def _s5_prep(a_re, a_im, log_dt, b_re, b_im):
    L, G, N = a_re.shape
    P = b_re.shape[-1]
    rows, cols = L * G, N * P
    exp_n = lambda a: jnp.broadcast_to(a[..., None], (L, G, N, P)).reshape(rows, cols)
    ldt = jnp.broadcast_to(log_dt[..., None, None], (L, G, N, P)).reshape(rows, cols)
    shp = jax.ShapeDtypeStruct((rows, cols), F32)
    spec = pl.BlockSpec((rows, cols), lambda: (0, 0))
    lr, li, bbr, bbi = pl.pallas_call(
        _s5_prep_kernel,
        in_specs=[spec] * 5,
        out_specs=[spec] * 4,
        out_shape=[shp] * 4,
        name="s5_discretise",
    )(exp_n(a_re), exp_n(a_im), ldt, b_re.reshape(rows, cols), b_im.reshape(rows, cols))
    lr = lr.reshape(L, G, N, P)[..., 0].reshape(L, G * N)
    li = li.reshape(L, G, N, P)[..., 0].reshape(L, G * N)
    return lr, li, bbr.reshape(L, G, N, P), bbi.reshape(L, G, N, P)


S5_GPB = LANES // SSM_GROUP
S5_NBLK = SSM_WIDTH // LANES
S5_SB = S5_GPB * SSM_STATE
S5_NS = SSM_GROUPS * SSM_STATE


def _s5_layout(lr, li, bbr, bbi, c_re, c_im, nbatch):
    eye = jnp.eye(S5_GPB, dtype=F32)

    def bmat(bb):
        bb = bb.reshape(S5_NBLK, S5_GPB, SSM_STATE, SSM_GROUP)
        m = jnp.einsum('jgnp,gh->jgphn', bb, eye)
        return m.reshape(S5_NBLK, LANES, S5_SB)

    def cmat(c):
        c = c.reshape(S5_NBLK, S5_GPB, SSM_GROUP, SSM_STATE)
        m = jnp.einsum('jgpn,gh->jgnhp', c, eye)
        return m.reshape(S5_NBLK, S5_SB, LANES)

    b_mat = jnp.concatenate([bmat(bbr), bmat(bbi)], axis=2).astype(BF16)
    c_mat = jnp.concatenate([cmat(c_re), -cmat(c_im)], axis=1).astype(BF16)
    lam = jnp.concatenate([jnp.broadcast_to(lr[None, :], (nbatch, S5_NS)),
                           jnp.broadcast_to(li[None, :], (nbatch, S5_NS))], axis=0)
    return b_mat, c_mat, lam


S5_SCAN_COLS = 1024


def _s5_kernel(u_ref, bmat_ref, lam_ref, cmat_ref, d_ref, wglu_ref, bglu_ref, o_ref,
               x_scr, h_scr, *, steps, nbatch):
    @pl.when(pl.program_id(0) == 0)
    def _():
        h_scr[...] = jnp.zeros_like(h_scr)

    u = u_ref[...]
    for j in range(S5_NBLK):
        xj = jnp.dot(u[:, j * LANES:(j + 1) * LANES], bmat_ref[j], preferred_element_type=F32)
        x_scr[:, j * S5_SB:(j + 1) * S5_SB] = xj[:, :S5_SB]
        x_scr[:, S5_NS + j * S5_SB:S5_NS + (j + 1) * S5_SB] = xj[:, S5_SB:]

    for c in range(S5_NS // S5_SCAN_COLS):
        re = slice(c * S5_SCAN_COLS, (c + 1) * S5_SCAN_COLS)
        im = slice(S5_NS + c * S5_SCAN_COLS, S5_NS + (c + 1) * S5_SCAN_COLS)
        lr = lam_ref[0:nbatch, re]
        li = lam_ref[nbatch:2 * nbatch, re]

        def body(t, carry, re=re, im=im, lr=lr, li=li):
            hr, hi = carry
            r0 = pl.multiple_of(t * nbatch, nbatch)
            nhr = lr * hr - li * hi + x_scr[pl.ds(r0, nbatch), re]
            nhi = lr * hi + li * hr + x_scr[pl.ds(r0, nbatch), im]
            x_scr[pl.ds(r0, nbatch), re] = nhr
            x_scr[pl.ds(r0, nbatch), im] = nhi
            return nhr, nhi

        hr, hi = lax.fori_loop(0, steps, body, (h_scr[:, re], h_scr[:, im]), unroll=4)
        h_scr[:, re] = hr
        h_scr[:, im] = hi

    ys = []
    for j in range(S5_NBLK):
        hr = x_scr[:, j * S5_SB:(j + 1) * S5_SB].astype(BF16)
        hi = x_scr[:, S5_NS + j * S5_SB:S5_NS + (j + 1) * S5_SB].astype(BF16)
        ys.append(jnp.dot(hr, cmat_ref[j, 0:S5_SB, :], preferred_element_type=F32)
                  + jnp.dot(hi, cmat_ref[j, S5_SB:2 * S5_SB, :], preferred_element_type=F32))
    y = jnp.concatenate(ys, axis=1) + d_ref[...] * u.astype(F32)
    y = jax.nn.gelu(y, approximate=True)
    z = jnp.dot(y.astype(BF16), wglu_ref[...], preferred_element_type=F32) + bglu_ref[...]
    o_ref[...] = (z[:, :SSM_WIDTH] * jax.nn.sigmoid(z[:, SSM_WIDTH:])).astype(o_ref.dtype)


def _s5(z, b_mat, lam, c_mat, d_skip, w_glu, b_glu, nbatch):
    T = z.shape[0]
    steps = 128
    R = steps * nbatch
    kern = functools.partial(_s5_kernel, steps=steps, nbatch=nbatch)
    const = lambda *shape: pl.BlockSpec(shape, lambda c: (0,) * len(shape))
    return pl.pallas_call(
        kern,
        grid=(T // R,),
        in_specs=[
            pl.BlockSpec((R, 512), lambda c: (c, COL_SSM)),
            const(S5_NBLK, LANES, 2 * S5_SB),
            const(2 * nbatch, S5_NS),
            const(S5_NBLK, 2 * S5_SB, LANES),
            const(1, SSM_WIDTH),
            const(SSM_WIDTH, 2 * SSM_WIDTH),
            const(1, 2 * SSM_WIDTH),
        ],
        out_specs=pl.BlockSpec((R, SSM_WIDTH), lambda c: (c, 0)),
        out_shape=jax.ShapeDtypeStruct((T, SSM_WIDTH), BF16),
        scratch_shapes=[pltpu.VMEM((R, 2 * S5_NS), F32), pltpu.VMEM((nbatch, 2 * S5_NS), F32)],
        compiler_params=_cparams("arbitrary"),
        name="s5_mixer",
    )(z, b_mat, lam, c_mat, d_skip, w_glu, b_glu)


def _merge_kernel(h_ref, ya_ref, yb_ref, yc_ref, g0a, g0b, g1a, g1b, g2a, g2b,
                  wbr_ref, wout_ref, lng_ref, lnb_ref, o_ref, yb_scr):
    nbatch, steps, _ = yb_ref.shape
    nslab = yb_scr.shape[0]
    for b in range(nbatch):
        for c in range(nslab):
            yb_scr[c, pl.ds(b, steps, stride=nbatch), :] = yb_ref[b, :, c * LANES:(c + 1) * LANES]
    yb = jnp.concatenate([yb_scr[c] for c in range(nslab)], axis=1).astype(BF16)
    ys = (ya_ref[...], yb, yc_ref[...])
    gs = ((g0a, g0b), (g1a, g1b), (g2a, g2b))
    half = D_MODEL // 2
    merged = []
    for hcol in range(2):
        acc = None
        for n in range(N_BRANCH):
            proj = jnp.dot(ys[n], wbr_ref[n, :, hcol * half:(hcol + 1) * half],
                           preferred_element_type=F32)
            term = jax.nn.sigmoid(gs[n][hcol][...].astype(F32)) * proj
            acc = term if acc is None else acc + term
        merged.append(acc.astype(BF16))
    merged = jnp.concatenate(merged, axis=1)
    mix = jnp.dot(merged, wout_ref[...], preferred_element_type=F32)
    r = DN_ALPHA * h_ref[...] + mix
    o_ref[...] = _layer_norm(r, lng_ref[...], lnb_ref[...])


def _merge(h, y_a, y_b, y_c, z, w_branch, w_out, ln_g, ln_b):
    T = h.shape[0]
    nbatch = y_b.shape[0]
    tm = 1024
    steps = tm // nbatch
    row = lambda w: pl.BlockSpec((tm, w), lambda i: (i, 0))
    gate = lambda k: pl.BlockSpec((tm, 512), lambda i, k=k: (i, COL_GATE + k))
    return pl.pallas_call(
        _merge_kernel,
        grid=(T // tm,),
        in_specs=[row(D_MODEL), row(512),
                  pl.BlockSpec((nbatch, steps, ATT_WIDTH), lambda i: (0, i, 0)),
                  row(512)]
        + [gate(k) for k in range(2 * N_BRANCH)]
        + [
            pl.BlockSpec((N_BRANCH, BRANCH_WIDTH, D_MODEL), lambda i: (0, 0, 0)),
            pl.BlockSpec((D_MODEL, D_MODEL), lambda i: (0, 0)),
            pl.BlockSpec((1, D_MODEL), lambda i: (0, 0)),
            pl.BlockSpec((1, D_MODEL), lambda i: (0, 0)),
        ],
        out_specs=row(D_MODEL),
        out_shape=jax.ShapeDtypeStruct((T, D_MODEL), F32),
        scratch_shapes=[pltpu.VMEM((ATT_WIDTH // LANES, tm, LANES), F32)],
        compiler_params=_cparams("arbitrary"),
        name="merge_ln",
    )(h, y_a, y_b, y_c, z, z, z, z, z, z, w_branch, w_out, ln_g, ln_b)


def _ffn_kernel(x_ref, w1_ref, w3_ref, w2_ref, lng_ref, lnb_ref, o_ref, xb_ref, acc_ref):
    f = pl.program_id(1)

    @pl.when(f == 0)
    def _():
        xb_ref[...] = x_ref[...].astype(BF16)
        acc_ref[...] = jnp.zeros_like(acc_ref)

    xb = xb_ref[...]
    a = jax.nn.silu(jnp.dot(xb, w1_ref[...], preferred_element_type=F32))
    a = a * jnp.dot(xb, w3_ref[...], preferred_element_type=F32)
    acc_ref[...] += jnp.dot(a.astype(BF16), w2_ref[...], preferred_element_type=F32)

    @pl.when(f == pl.num_programs(1) - 1)
    def _():
        r = DN_ALPHA * x_ref[...] + acc_ref[...]
        o_ref[...] = _layer_norm(r, lng_ref[...], lnb_ref[...])


FFN_TF = 1408


def _ffn(x, w1, w3, w2, ln_g, ln_b):
    T = x.shape[0]
    tm, tf = 512, FFN_TF
    return pl.pallas_call(
        _ffn_kernel,
        grid=(T // tm, D_FF // tf),
        in_specs=[
            pl.BlockSpec((tm, D_MODEL), lambda i, f: (i, 0)),
            pl.BlockSpec((D_MODEL, tf), lambda i, f: (0, f)),
            pl.BlockSpec((D_MODEL, tf), lambda i, f: (0, f)),
            pl.BlockSpec((tf, D_MODEL), lambda i, f: (f, 0)),
            pl.BlockSpec((1, D_MODEL), lambda i, f: (0, 0)),
            pl.BlockSpec((1, D_MODEL), lambda i, f: (0, 0)),
        ],
        out_specs=pl.BlockSpec((tm, D_MODEL), lambda i, f: (i, 0)),
        out_shape=jax.ShapeDtypeStruct((T, D_MODEL), F32),
        scratch_shapes=[pltpu.VMEM((tm, D_MODEL), BF16), pltpu.VMEM((tm, D_MODEL), F32)],
        compiler_params=_cparams("arbitrary", "arbitrary"),
        name="ffn_ln",
    )(x, w1, w3, w2, ln_g, ln_b)


def _router_kernel(x_ref, w_ref, b_ref, idx_ref, wgt_ref):
    logits = jnp.dot(x_ref[...], w_ref[...], preferred_element_type=F32,
                     precision=lax.Precision.HIGHEST) + b_ref[...]
    lane = lax.broadcasted_iota(jnp.int32, logits.shape, 1)
    l1 = jnp.where(lane < N_EXPERTS, logits, -jnp.inf)
    m1 = jnp.max(l1, axis=1, keepdims=True)
    i1 = jnp.min(jnp.where(l1 == m1, lane, LANES), axis=1, keepdims=True)
    l2 = jnp.where(lane == i1, -jnp.inf, l1)
    m2 = jnp.max(l2, axis=1, keepdims=True)
    i2 = jnp.min(jnp.where(l2 == m2, lane, LANES), axis=1, keepdims=True)
    e = jnp.exp(m2 - m1)
    w_top = 1.0 / (1.0 + e)
    w_sec = e / (1.0 + e)
    idx_ref[...] = jnp.where(lane == 0, i1, jnp.where(lane == 1, i2, 0))
    wgt_ref[...] = jnp.where(lane == 0, w_top, jnp.where(lane == 1, w_sec, 0.0))


def _router(x, w, b):
    T = x.shape[0]
    tm = 1024
    out = pl.BlockSpec((tm, LANES), lambda i: (i, 0))
    return pl.pallas_call(
        _router_kernel,
        grid=(T // tm,),
        in_specs=[
            pl.BlockSpec((tm, D_MODEL), lambda i: (i, 0)),
            pl.BlockSpec((D_MODEL, LANES), lambda i: (0, 0)),
            pl.BlockSpec((1, LANES), lambda i: (0, 0)),
        ],
        out_specs=[out, out],
        out_shape=[jax.ShapeDtypeStruct((T, LANES), jnp.int32),
                   jax.ShapeDtypeStruct((T, LANES), F32)],
        compiler_params=_cparams("arbitrary"),
        name="moe_router",
    )(x, w, b)


MOE_TM = 512
MOE_DMA_UNROLL = 8


def _route_plan(idx, wgt, T):
    n_assign = TOP_K * T
    n_tiles = n_assign // MOE_TM + N_EXPERTS
    e_flat = idx.T.reshape(-1)
    w_flat = wgt.T.reshape(-1)
    order = jnp.argsort(e_flat, stable=True).astype(jnp.int32)
    counts = jnp.sum((e_flat[:, None] == jnp.arange(N_EXPERTS)[None, :]).astype(jnp.int32), axis=0)
    group_start = jnp.cumsum(counts) - counts
    tiles_per = (counts + MOE_TM - 1) // MOE_TM
    tile_end = jnp.cumsum(tiles_per)
    tile_start = tile_end - tiles_per
    tile_id = jnp.arange(n_tiles, dtype=jnp.int32)
    used = tile_id < tile_end[-1]
    tile_expert = jnp.minimum(jnp.sum((tile_id[:, None] >= tile_end[None, :]).astype(jnp.int32), axis=1),
                              N_EXPERTS - 1)
    tile_first = (tile_id - tile_start[tile_expert]) * MOE_TM
    tile_rows = jnp.where(used, jnp.clip(counts[tile_expert] - tile_first, 0, MOE_TM), 0)
    last_expert = tile_expert[jnp.maximum(tile_end[-1] - 1, 0)]
    tile_expert = jnp.where(used, tile_expert, last_expert).astype(jnp.int32)
    j = jnp.arange(MOE_TM, dtype=jnp.int32)[None, :]
    valid = j < tile_rows[:, None]
    pos = jnp.clip(group_start[tile_expert][:, None] + tile_first[:, None] + j, 0, n_assign - 1)
    a = jnp.where(valid, order[pos], 0)
    src_tok = jnp.where(a >= T, a - T, a).astype(jnp.int32)
    row_w = jnp.where(valid, w_flat[a], 0.0)
    return (tile_expert, tile_rows.astype(jnp.int32), src_tok[:, None, :], a.astype(jnp.int32)[:, None, :],
            row_w[:, :, None])


def _moe_kernel(te_ref, tn_ref, src_ref, src_next_ref, dst_ref, roww_ref, x_hbm, w1_ref, w3_ref, w2_ref,
                buf_hbm, xg, xb, acc, y_scr, gsem, ssem):
    t = pl.program_id(0)
    f = pl.program_id(1)
    n_tiles = pl.num_programs(0)
    last_f = pl.num_programs(1) - 1
    nrows = tn_ref[t]
    slot = t % 2

    def start_gather(idx_ref, buf_slot, r):
        pltpu.make_async_copy(x_hbm.at[pl.ds(idx_ref[0, r], 1), :], xg.at[buf_slot, pl.ds(r, 1), :],
                              gsem.at[buf_slot]).start()

    def start_scatter(r):
        pltpu.make_async_copy(y_scr.at[pl.ds(r, 1), :], buf_hbm.at[pl.ds(dst_ref[0, r], 1), :], ssem).start()

    def for_rows(n, fn):
        ngroups = n // MOE_DMA_UNROLL

        def group(g, c):
            base = pl.multiple_of(g * MOE_DMA_UNROLL, MOE_DMA_UNROLL)
            for u in range(MOE_DMA_UNROLL):
                fn(base + u)
            return c

        def single(r, c):
            fn(r)
            return c

        lax.fori_loop(0, ngroups, group, 0)
        lax.fori_loop(ngroups * MOE_DMA_UNROLL, n, single, 0)

    def wait_rows(n, src, dst, sem):
        p = MOE_TM
        while p >= 1:
            @pl.when((n & p) != 0)
            def _(p=p):
                pltpu.make_async_copy(src.at[pl.ds(0, p), :], dst.at[pl.ds(0, p), :], sem).wait()
            p //= 2

    @pl.when((t == 0) & (f == 0))
    def _():
        xg[...] = jnp.zeros_like(xg)
        for_rows(nrows, functools.partial(start_gather, src_ref, 0))

    @pl.when(f == 0)
    def _():
        wait_rows(nrows, x_hbm, xg.at[slot], gsem.at[slot])

        @pl.when(nrows > 0)
        def _():
            xb[...] = xg[slot].astype(BF16)

        @pl.when(t + 1 < n_tiles)
        def _():
            for_rows(tn_ref[t + 1], functools.partial(start_gather, src_next_ref, 1 - slot))

    @pl.when(nrows > 0)
    def _():
        x = xb[...]
        a = jax.nn.silu(jnp.dot(x, w1_ref[...], preferred_element_type=F32))
        a = a * jnp.dot(x, w3_ref[...], preferred_element_type=F32)
        y = jnp.dot(a.astype(BF16), w2_ref[...], preferred_element_type=F32)

        @pl.when(f == 0)
        def _():
            acc[...] = y

        @pl.when(f > 0)
        def _():
            acc[...] += y

    @pl.when(f == last_f)
    def _():
        @pl.when(t > 0)
        def _():
            wait_rows(tn_ref[t - 1], y_scr, buf_hbm, ssem)

        @pl.when(nrows > 0)
        def _():
            y_scr[...] = acc[...] * roww_ref[...]
            for_rows(nrows, start_scatter)

        @pl.when(t == n_tiles - 1)
        def _():
            wait_rows(nrows, y_scr, buf_hbm, ssem)


def _moe_experts(x, plan, w1, w3, w2):
    T = x.shape[0]
    tile_expert, tile_rows, src_tok, dst_row, row_w = plan
    n_tiles = tile_expert.shape[0]
    tf = FFN_TF
    smem_rows = pl.BlockSpec((None, 1, MOE_TM), lambda t, f, te, tn: (t, 0, 0), memory_space=pltpu.SMEM)
    smem_next = pl.BlockSpec((None, 1, MOE_TM), lambda t, f, te, tn: (jnp.minimum(t + 1, n_tiles - 1), 0, 0),
                             memory_space=pltpu.SMEM)
    grid_spec = pltpu.PrefetchScalarGridSpec(
        num_scalar_prefetch=2,
        grid=(n_tiles, D_FF // tf),
        in_specs=[
            smem_rows,
            smem_next,
            smem_rows,
            pl.BlockSpec((None, MOE_TM, 1), lambda t, f, te, tn: (t, 0, 0)),
            pl.BlockSpec(memory_space=pl.ANY),
            pl.BlockSpec((None, D_MODEL, tf), lambda t, f, te, tn: (te[t], 0, f)),
            pl.BlockSpec((None, D_MODEL, tf), lambda t, f, te, tn: (te[t], 0, f)),
            pl.BlockSpec((None, tf, D_MODEL), lambda t, f, te, tn: (te[t], f, 0)),
        ],
        out_specs=pl.BlockSpec(memory_space=pl.ANY),
        scratch_shapes=[
            pltpu.VMEM((2, MOE_TM, D_MODEL), F32),
            pltpu.VMEM((MOE_TM, D_MODEL), BF16),
            pltpu.VMEM((MOE_TM, D_MODEL), F32),
            pltpu.VMEM((MOE_TM, D_MODEL), F32),
            pltpu.SemaphoreType.DMA((2,)),
            pltpu.SemaphoreType.DMA(()),
        ],
    )
    return pl.pallas_call(
        _moe_kernel,
        grid_spec=grid_spec,
        out_shape=jax.ShapeDtypeStruct((TOP_K * T, D_MODEL), F32),
        compiler_params=_cparams("arbitrary", "arbitrary"),
        name="moe_experts",
    )(tile_expert, tile_rows, src_tok, src_tok, dst_row, row_w, x, w1, w3, w2)


def _moe_combine_kernel(x_ref, y0_ref, y1_ref, lng_ref, lnb_ref, o_ref):
    r = DN_ALPHA * x_ref[...] + (y0_ref[...] + y1_ref[...])
    o_ref[...] = _layer_norm(r, lng_ref[...], lnb_ref[...])


def _moe_combine(x, buf, ln_g, ln_b):
    T = x.shape[0]
    tm = 1024
    nblk = T // tm
    return pl.pallas_call(
        _moe_combine_kernel,
        grid=(nblk,),
        in_specs=[
            pl.BlockSpec((tm, D_MODEL), lambda i: (i, 0)),
            pl.BlockSpec((tm, D_MODEL), lambda i: (i, 0)),
            pl.BlockSpec((tm, D_MODEL), lambda i: (i + nblk, 0)),
            pl.BlockSpec((1, D_MODEL), lambda i: (0, 0)),
            pl.BlockSpec((1, D_MODEL), lambda i: (0, 0)),
        ],
        out_specs=pl.BlockSpec((tm, D_MODEL), lambda i: (i, 0)),
        out_shape=jax.ShapeDtypeStruct((T, D_MODEL), F32),
        compiler_params=_cparams("arbitrary"),
        name="moe_combine_ln",
    )(x, buf, buf, ln_g, ln_b)


def kernel(x, ln1_g, ln1_b, w_in, b_in, w_pool, pool_scale, ssm_a_re, ssm_a_im, ssm_log_dt,
           ssm_b_re, ssm_b_im, ssm_c_re, ssm_c_im, ssm_d, w_glu, b_glu, w_branch, w_out,
           ln2_g, ln2_b, ffn_w1, ffn_w3, ffn_w2, moe_router, moe_router_b, moe_w1, moe_w3, moe_w2):
    B, S, D = x.shape
    T = B * S
    assert B == SUBLANES and D == D_MODEL
    h = x.transpose(1, 0, 2).reshape(T, D)

    lr, li, bbr, bbi = _s5_prep(ssm_a_re, ssm_a_im, ssm_log_dt, ssm_b_re, ssm_b_im)
    row = lambda a: a.reshape(1, -1)

    for i in range(DEPTH):
        qkv, z = _inproj(h, _to_bf16(w_in, (i,)), row(b_in[i]), B)
        y_b = _attention(qkv)
        y_a = _pool(z, w_pool[i].astype(BF16), row(pool_scale[i]), B)
        b_mat, c_mat, lam = _s5_layout(lr[i], li[i], bbr[i], bbi[i], ssm_c_re[i], ssm_c_im[i], B)
        y_c = _s5(z, b_mat, lam, c_mat, row(ssm_d[i]), w_glu[i].astype(BF16), row(b_glu[i]), B)
        h = _merge(h, y_a, y_b, y_c, z, w_branch[i].astype(BF16), w_out[i].astype(BF16),
                   row(ln1_g[i]), row(ln1_b[i]))
        j = i // 2
        if i % 2 == 0:
            h = _ffn(h, _to_bf16(ffn_w1, (j,)), _to_bf16(ffn_w3, (j,)), _to_bf16(ffn_w2, (j,)),
                     row(ln2_g[i]), row(ln2_b[i]))
        else:
            rw = jnp.pad(moe_router[j], ((0, 0), (0, LANES - N_EXPERTS)))
            rb = jnp.pad(moe_router_b[j], (0, LANES - N_EXPERTS)).reshape(1, LANES)
            idx, wgt = _router(h, rw, rb)
            plan = _route_plan(idx[:, :TOP_K], wgt[:, :TOP_K], T)
            buf = _moe_experts(h, plan, _to_bf16(moe_w1, (j,)), _to_bf16(moe_w3, (j,)),
                               _to_bf16(moe_w2, (j,)))
            h = _moe_combine(h, buf, row(ln2_g[i]), row(ln2_b[i]))
    return h.reshape(S, B, D).transpose(1, 0, 2)
```
